```python
import math
import jax, jax.numpy as jnp
from jax import lax
import numpy as np

D_MODEL = 1024
BATCH = 8
SEQ = 4096
DEPTH = 1
DEC_BATCH = 128
DEC_SEQ = 1
PAST_LEN = 16384
PAGE_SIZE = 128

N_META = 16
N_HEADS = 16
N_KV_HEADS = 4
HEAD_DIM = 64
Q_PER_KV = N_HEADS // N_KV_HEADS
WINDOW = 128
BLOCK = 128
META_PAD = BLOCK - N_META
ROPE_THETA = 10000.0
SSM_INNER = D_MODEL
SSM_HEAD_DIM = 64
SSM_HEADS = SSM_INNER // SSM_HEAD_DIM
SSM_GROUPS = 4
HEADS_PER_GROUP = SSM_HEADS // SSM_GROUPS
SSM_STATE = 128
CONV_WIDTH = 4
CONV_DIM = SSM_INNER + 2 * SSM_GROUPS * SSM_STATE
Q_DIM = N_HEADS * HEAD_DIM
KV_DIM = N_KV_HEADS * HEAD_DIM
IN_DIM = Q_DIM + 2 * KV_DIM + SSM_INNER + CONV_DIM + SSM_HEADS + 2 * D_MODEL
N_EXPERT_GROUPS = 4
EXPERTS_PER_GROUP = 8
N_EXPERTS = N_EXPERT_GROUPS * EXPERTS_PER_GROUP
TOP_K_IN_GROUP = 2
D_EXPERT = 256
ALPHA = (2 * DEPTH) ** 0.25
BETA = (8 * DEPTH) ** -0.25
LN_EPS = 1e-5
RMS_EPS = 1e-5
NEG_INF = -1e30

kernel_name = 'hybrid_swa_ssd_hmoe_decoder_step'


def layer_norm(x, g, b):
    xf = x.astype(jnp.float32)
    mu = jnp.mean(xf, -1, keepdims=True)
    var = jnp.mean(jnp.square(xf - mu), -1, keepdims=True)
    y = (xf - mu) * lax.rsqrt(var + LN_EPS)
    return (y * g.astype(jnp.float32) + b.astype(jnp.float32)).astype(x.dtype)


def rope(x, pos):
    half = HEAD_DIM // 2
    inv_freq = jnp.exp(-math.log(ROPE_THETA) * jnp.arange(half, dtype=jnp.float32) * (2.0 / HEAD_DIM))
    ang = pos.astype(jnp.float32)[:, None] * inv_freq[None, :]
    cos = jnp.cos(ang)[:, None, :]
    sin = jnp.sin(ang)[:, None, :]
    xf = x.astype(jnp.float32)
    x1, x2 = xf[..., :half], xf[..., half:]
    return jnp.concatenate([x1 * cos - x2 * sin, x2 * cos + x1 * sin], -1).astype(x.dtype)


def split_in(u):
    sizes = (Q_DIM, KV_DIM, KV_DIM, SSM_INNER, CONV_DIM, SSM_HEADS, D_MODEL, D_MODEL)
    parts, start = [], 0
    for s in sizes:
        parts.append(u[..., start:start + s])
        start += s
    return parts


def split_xbc(t):
    gn = SSM_GROUPS * SSM_STATE
    return t[..., :SSM_INNER], t[..., SSM_INNER:SSM_INNER + gn], t[..., SSM_INNER + gn:]


def causal_conv(hist, xbc, w, b):
    s = xbc.shape[1]
    full = jnp.concatenate([hist.astype(xbc.dtype), xbc], axis=1)
    y = b
    for j in range(CONV_WIDTH):
        y = y + full[:, j:j + s] * w[j]
    return jax.nn.silu(y), full[:, -(CONV_WIDTH - 1):]


def gated_rmsnorm(y, z, w):
    yf = y.astype(jnp.float32) * jax.nn.silu(z.astype(jnp.float32))
    shp = yf.shape
    yg = yf.reshape(shp[:-1] + (SSM_GROUPS, SSM_INNER // SSM_GROUPS))
    yg = yg * lax.rsqrt(jnp.mean(jnp.square(yg), -1, keepdims=True) + RMS_EPS)
    return yg.reshape(shp) * w.astype(jnp.float32)


def segsum(a):
    t = a.shape[-1]
    cs = jnp.cumsum(a, -1)
    diff = cs[..., :, None] - cs[..., None, :]
    return jnp.where(jnp.tril(jnp.ones((t, t), dtype=bool)), diff, -jnp.inf)


def ssd_chunked(x, dt, a, bm, cm):
    b, seq = x.shape[:2]
    nc = seq // BLOCK
    xd = (x * dt[..., None]).reshape(b, nc, BLOCK, SSM_GROUPS, HEADS_PER_GROUP, SSM_HEAD_DIM)
    ad = (dt * a).reshape(b, nc, BLOCK, SSM_GROUPS, HEADS_PER_GROUP).transpose(0, 3, 4, 1, 2)
    bc = bm.reshape(b, nc, BLOCK, SSM_GROUPS, SSM_STATE)
    cc = cm.reshape(b, nc, BLOCK, SSM_GROUPS, SSM_STATE)
    a_cs = jnp.cumsum(ad, -1)
    lmat = jnp.exp(segsum(ad))
    cb = jnp.einsum('bclgn,bcsgn->bgcls', cc, bc)
    y_diag = jnp.einsum('bgcls,bgrcls,bcsgrp->bclgrp', cb, lmat, xd)
    decay_states = jnp.exp(a_cs[..., -1:] - a_cs)
    states = jnp.einsum('bclgn,bgrcl,bclgrp->bcgrpn', bc, decay_states, xd)
    states = jnp.concatenate([jnp.zeros_like(states[:, :1]), states], axis=1)
    chunk_tot = jnp.pad(a_cs[..., -1], ((0, 0), (0, 0), (0, 0), (1, 0)))
    decay_chunk = jnp.exp(segsum(chunk_tot))
    new_states = jnp.einsum('bgrzc,bcgrpn->bzgrpn', decay_chunk, states)
    states, final = new_states[:, :-1], new_states[:, -1]
    y_off = jnp.einsum('bclgn,bcgrpn,bgrcl->bclgrp', cc, states, jnp.exp(a_cs))
    y = (y_diag + y_off).reshape(b, seq, SSM_HEADS, SSM_HEAD_DIM)
    return y, final.reshape(b, SSM_HEADS, SSM_HEAD_DIM, SSM_STATE)


def ssd_recurrent(h0, x, dt, a, bm, cm):
    def step(h, inp):
        x_t, dt_t, b_t, c_t = inp
        b_h = jnp.repeat(b_t, HEADS_PER_GROUP, axis=1)
        c_h = jnp.repeat(c_t, HEADS_PER_GROUP, axis=1)
        h = h * jnp.exp(dt_t * a)[:, :, None, None] + jnp.einsum('nhp,nhs->nhps', x_t * dt_t[:, :, None], b_h)
        return h, jnp.einsum('nhps,nhs->nhp', h, c_h)
    seqs = (jnp.swapaxes(x, 0, 1), jnp.swapaxes(dt, 0, 1), jnp.swapaxes(bm, 0, 1), jnp.swapaxes(cm, 0, 1))
    h_last, ys = lax.scan(step, h0, seqs)
    return jnp.swapaxes(ys, 0, 1), h_last


def softmax_with_sink(logits, sinks):
    sink = jnp.broadcast_to(sinks.astype(jnp.float32).reshape(N_KV_HEADS, Q_PER_KV)[:, :, None, None],
                            logits.shape[:-1] + (1,))
    probs = jax.nn.softmax(jnp.concatenate([logits, sink], axis=-1), axis=-1)
    return probs[..., :-1]


def banded_attention(q, k, v, meta_k, meta_v, pos, sinks):
    b, lp = q.shape[:2]
    nb = lp // BLOCK
    scale = HEAD_DIM ** -0.5
    qb = q.reshape(b, nb, BLOCK, N_KV_HEADS, Q_PER_KV, HEAD_DIM)
    kb = k.reshape(b, nb, BLOCK, N_KV_HEADS, HEAD_DIM)
    vb = v.reshape(b, nb, BLOCK, N_KV_HEADS, HEAD_DIM)
    prev = lambda t: jnp.concatenate([jnp.zeros_like(t[:, :1]), t[:, :-1]], axis=1)
    k_band = jnp.concatenate([prev(kb), kb], axis=2)
    v_band = jnp.concatenate([prev(vb), vb], axis=2)
    pos_b = pos.reshape(nb, BLOCK)
    pos_prev = jnp.concatenate([jnp.full((1, BLOCK), -1, jnp.int32), pos_b[:-1]], axis=0)
    pk = jnp.concatenate([pos_prev, pos_b], axis=1)[:, None, :]
    pq = pos_b[:, :, None]
    band_mask = (pk >= N_META) & (pk <= pq) & (pq - pk < WINDOW)
    meta_mask = jnp.arange(N_META, dtype=jnp.int32)[None, None, :] <= pq
    s_band = jnp.einsum('bnqkgd,bnskd->bnkgqs', qb, k_band).astype(jnp.float32) * scale
    s_meta = jnp.einsum('bnqkgd,bmkd->bnkgqm', qb, meta_k).astype(jnp.float32) * scale
    s_band = jnp.where(band_mask[None, :, None, None], s_band, NEG_INF)
    s_meta = jnp.where(meta_mask[None, :, None, None], s_meta, NEG_INF)
    probs = softmax_with_sink(jnp.concatenate([s_band, s_meta], axis=-1), sinks).astype(v.dtype)
    out = (jnp.einsum('bnkgqs,bnskd->bnqkgd', probs[..., :2 * BLOCK], v_band)
           + jnp.einsum('bnkgqm,bmkd->bnqkgd', probs[..., 2 * BLOCK:], meta_v))
    return out.reshape(b, lp, Q_DIM)


def window_attention_step(q, k_new, v_new, meta_k, meta_v, win_k, win_v, sinks):
    n, t = q.shape[:2]
    w = win_k.shape[1]
    scale = HEAD_DIM ** -0.5
    qg = q.reshape(n, t, N_KV_HEADS, Q_PER_KV, HEAD_DIM)
    keys = jnp.concatenate([meta_k.astype(k_new.dtype), win_k.astype(k_new.dtype), k_new], axis=1)
    vals = jnp.concatenate([meta_v.astype(v_new.dtype), win_v.astype(v_new.dtype), v_new], axis=1)
    pq = (PAST_LEN + jnp.arange(t, dtype=jnp.int32))[:, None]
    pk = jnp.concatenate([jnp.arange(N_META, dtype=jnp.int32),
                          PAST_LEN - w + jnp.arange(w, dtype=jnp.int32),
                          PAST_LEN + jnp.arange(t, dtype=jnp.int32)])
    is_meta = (jnp.arange(pk.shape[0]) < N_META)[None, :]
    pk = pk[None, :]
    mask = (pk <= pq) & (is_meta | ((pk >= N_META) & (pq - pk < WINDOW)))
    s = jnp.einsum('ntkgd,nskd->nkgts', qg, keys).astype(jnp.float32) * scale
    s = jnp.where(mask[None, None, None], s, NEG_INF)
    probs = softmax_with_sink(s, sinks).astype(vals.dtype)
    return jnp.einsum('nkgts,nskd->ntkgd', probs, vals).reshape(n, t, Q_DIM)


def merge_branches(y_attn, y_ssd, g_a, g_s, w_o):
    return (jax.nn.sigmoid(g_a) * y_attn + jax.nn.sigmoid(g_s) * y_ssd) @ w_o


def mixer_prompt(h, pos, valid, w_in, attn_sinks, conv_w, conv_b, dt_bias, a_log, d_skip, ssm_norm_w, w_o):
    b, lp, _ = h.shape
    q, k, v, z, xbc, dt_raw, g_a, g_s = split_in(h @ w_in)
    q = rope(q.reshape(b, lp, N_HEADS, HEAD_DIM), pos)
    k = rope(k.reshape(b, lp, N_KV_HEADS, HEAD_DIM), pos)
    v = v.reshape(b, lp, N_KV_HEADS, HEAD_DIM)
    meta_k, meta_v = k[:, META_PAD:BLOCK], v[:, META_PAD:BLOCK]
    y_attn = banded_attention(q, k, v, meta_k, meta_v, pos, attn_sinks)
    xbc_c, conv_state = causal_conv(jnp.zeros((b, CONV_WIDTH - 1, CONV_DIM), xbc.dtype), xbc, conv_w, conv_b)
    xs, bm, cm = split_xbc(xbc_c)
    dt = jnp.where(valid[None, :, None],
                   jax.nn.softplus(dt_raw.astype(jnp.float32) + dt_bias.astype(jnp.float32)), 0.0)
    a = -jnp.exp(a_log.astype(jnp.float32))
    xs_h = xs.reshape(b, lp, SSM_HEADS, SSM_HEAD_DIM).astype(jnp.float32)
    y, ssm_state = ssd_chunked(xs_h, dt, a,
                               bm.reshape(b, lp, SSM_GROUPS, SSM_STATE).astype(jnp.float32),
                               cm.reshape(b, lp, SSM_GROUPS, SSM_STATE).astype(jnp.float32))
    y = y + d_skip.astype(jnp.float32)[:, None] * xs_h
    y_ssd = gated_rmsnorm(y.reshape(b, lp, SSM_INNER), z, ssm_norm_w).astype(h.dtype)
    mix = merge_branches(y_attn, y_ssd, g_a, g_s, w_o)
    return (mix, meta_k, meta_v, k[:, -WINDOW:], v[:, -WINDOW:], conv_state, ssm_state.astype(h.dtype))


def mixer_sample(h, meta_k, meta_v, win_k, win_v, conv_state, ssm_state,
                 w_in, attn_sinks, conv_w, conv_b, dt_bias, a_log, d_skip, ssm_norm_w, w_o):
    n, t, _ = h.shape
    pos = PAST_LEN + jnp.arange(t, dtype=jnp.int32)
    q, k, v, z, xbc, dt_raw, g_a, g_s = split_in(h @ w_in)
    q = rope(q.reshape(n, t, N_HEADS, HEAD_DIM), pos)
    k = rope(k.reshape(n, t, N_KV_HEADS, HEAD_DIM), pos)
    v = v.reshape(n, t, N_KV_HEADS, HEAD_DIM)
    y_attn = window_attention_step(q, k, v, meta_k, meta_v, win_k, win_v, attn_sinks)
    xbc_c, new_conv = causal_conv(conv_state, xbc, conv_w, conv_b)
    xs, bm, cm = split_xbc(xbc_c)
    dt = jax.nn.softplus(dt_raw.astype(jnp.float32) + dt_bias.astype(jnp.float32))
    a = -jnp.exp(a_log.astype(jnp.float32))
    xs_h = xs.reshape(n, t, SSM_HEADS, SSM_HEAD_DIM).astype(jnp.float32)
    y, new_ssm = ssd_recurrent(ssm_state.astype(jnp.float32), xs_h, dt, a,
                               bm.reshape(n, t, SSM_GROUPS, SSM_STATE).astype(jnp.float32),
                               cm.reshape(n, t, SSM_GROUPS, SSM_STATE).astype(jnp.float32))
    y = y + d_skip.astype(jnp.float32)[:, None] * xs_h
    y_ssd = gated_rmsnorm(y.reshape(n, t, SSM_INNER), z, ssm_norm_w).astype(h.dtype)
    mix = merge_branches(y_attn, y_ssd, g_a, g_s, w_o)
    return (mix, k, v, new_conv, new_ssm.astype(ssm_state.dtype))


def hier_moe(h, rg_w, rg_b, re_w, re_b, w_gate, w_up, w_down):
    shp = h.shape
    x = h.reshape(-1, D_MODEL)
    g_logits = (x @ rg_w).astype(jnp.float32) + rg_b.astype(jnp.float32)
    g_prob = jax.nn.softmax(g_logits, axis=-1)
    g_sel = jnp.argmax(g_logits, axis=-1)
    g_w = jnp.take_along_axis(g_prob, g_sel[:, None], axis=-1)[:, 0]
    e_logits = ((x @ re_w).astype(jnp.float32) + re_b.astype(jnp.float32)).reshape(-1, N_EXPERT_GROUPS, EXPERTS_PER_GROUP)
    e_in = jnp.take_along_axis(e_logits, g_sel[:, None, None], axis=1)[:, 0]
    top_v, top_i = lax.top_k(e_in, TOP_K_IN_GROUP)
    top_p = jax.nn.softmax(top_v, axis=-1) * g_w[:, None]
    comb = jnp.sum(jax.nn.one_hot(top_i, EXPERTS_PER_GROUP, dtype=jnp.float32) * top_p[..., None], axis=1)
    y = jnp.zeros_like(x)
    for g in range(N_EXPERT_GROUPS):
        sl = slice(g * EXPERTS_PER_GROUP, (g + 1) * EXPERTS_PER_GROUP)
        hid = jax.nn.silu(jnp.einsum('md,edf->mef', x, w_gate[sl])) * jnp.einsum('md,edf->mef', x, w_up[sl])
        wg = jnp.where((g_sel == g)[:, None], comb, 0.0).astype(x.dtype)
        y = y + jnp.einsum('mef,me,efd->md', hid, wg, w_down[sl])
    return y.reshape(shp)


def setup_inputs(seed: int = 0) -> dict:
    key = jax.random.key(seed)
    ks = jax.random.split(key, 32)
    nrm = lambda k, shape, s=1.0: jax.random.normal(k, shape, jnp.float32) * s
    w_buf = min(WINDOW, PAST_LEN)
    dt0 = jnp.exp(jax.random.uniform(ks[12], (DEPTH, SSM_HEADS), jnp.float32, math.log(1e-3), math.log(1e-1)))
    dt_bias = dt0 + jnp.log(-jnp.expm1(-dt0))
    a_log = jnp.log(jax.random.uniform(ks[13], (DEPTH, SSM_HEADS), jnp.float32, 1.0, 16.0))
    return {
        'x_prompt': nrm(ks[0], (BATCH, SEQ, D_MODEL)),
        'x_sample': nrm(ks[1], (DEC_BATCH, DEC_SEQ, D_MODEL)),
        'cache_meta_k': nrm(ks[2], (DEPTH, DEC_BATCH, N_META, N_KV_HEADS, HEAD_DIM)),
        'cache_meta_v': nrm(ks[3], (DEPTH, DEC_BATCH, N_META, N_KV_HEADS, HEAD_DIM)),
        'cache_win_k': nrm(ks[4], (DEPTH, DEC_BATCH, w_buf, N_KV_HEADS, HEAD_DIM)),
        'cache_win_v': nrm(ks[5], (DEPTH, DEC_BATCH, w_buf, N_KV_HEADS, HEAD_DIM)),
        'state_conv': nrm(ks[6], (DEPTH, DEC_BATCH, CONV_WIDTH - 1, CONV_DIM)),
        'state_ssm': nrm(ks[7], (DEPTH, DEC_BATCH, SSM_HEADS, SSM_HEAD_DIM, SSM_STATE), 0.5),
        'meta_tokens': nrm(ks[8], (N_META, D_MODEL)),
        'w_in': nrm(ks[9], (DEPTH, D_MODEL, IN_DIM), D_MODEL ** -0.5),
        'attn_sinks': nrm(ks[10], (DEPTH, N_HEADS)),
        'conv_w': nrm(ks[11], (DEPTH, CONV_WIDTH, CONV_DIM), CONV_WIDTH ** -0.5),
        'conv_b': nrm(ks[14], (DEPTH, CONV_DIM), 0.02),
        'dt_bias': dt_bias,
        'a_log': a_log,
        'd_skip': 1.0 + nrm(ks[15], (DEPTH, SSM_HEADS), 0.1),
        'ssm_norm_w': 1.0 + nrm(ks[16], (DEPTH, SSM_INNER), 0.1),
        'w_o': nrm(ks[17], (DEPTH, D_MODEL, D_MODEL), BETA * D_MODEL ** -0.5),
        'ln1_g': 1.0 + nrm(ks[18], (DEPTH, D_MODEL), 0.1),
        'ln1_b': nrm(ks[19], (DEPTH, D_MODEL), 0.02),
        'router_group_w': nrm(ks[20], (DEPTH, D_MODEL, N_EXPERT_GROUPS), D_MODEL ** -0.5),
        'router_group_b': nrm(ks[21], (DEPTH, N_EXPERT_GROUPS), 0.01),
        'router_expert_w': nrm(ks[22], (DEPTH, D_MODEL, N_EXPERTS), D_MODEL ** -0.5),
        'router_expert_b': nrm(ks[23], (DEPTH, N_EXPERTS), 0.01),
        'w_gate': nrm(ks[24], (DEPTH, N_EXPERTS, D_MODEL, D_EXPERT), D_MODEL ** -0.5),
        'w_up': nrm(ks[25], (DEPTH, N_EXPERTS, D_MODEL, D_EXPERT), D_MODEL ** -0.5),
        'w_down': nrm(ks[26], (DEPTH, N_EXPERTS, D_EXPERT, D_MODEL), BETA * D_EXPERT ** -0.5),
        'ln2_g': 1.0 + nrm(ks[27], (DEPTH, D_MODEL), 0.1),
        'ln2_b': nrm(ks[28], (DEPTH, D_MODEL), 0.02),
    }


def reference(x_prompt, x_sample, cache_meta_k, cache_meta_v, cache_win_k, cache_win_v, state_conv, state_ssm,
              meta_tokens, w_in, attn_sinks, conv_w, conv_b, dt_bias, a_log, d_skip, ssm_norm_w, w_o,
              ln1_g, ln1_b, router_group_w, router_group_b, router_expert_w, router_expert_b,
              w_gate, w_up, w_down, ln2_g, ln2_b):
    b = x_prompt.shape[0]
    lp = BLOCK + x_prompt.shape[1]
    pos = jnp.arange(lp, dtype=jnp.int32) - META_PAD
    valid = pos >= 0
    meta = jnp.broadcast_to(meta_tokens.astype(x_prompt.dtype)[None], (b, N_META, D_MODEL))
    hp = jnp.concatenate([jnp.zeros((b, META_PAD, D_MODEL), x_prompt.dtype), meta, x_prompt], axis=1)
    hs = x_sample
    pmk, pmv, pwk, pwv, pcv, pss, swk, swv, scv, sss = [], [], [], [], [], [], [], [], [], []
    for l in range(DEPTH):
        hp = jnp.where(valid[None, :, None], hp, 0.0).astype(x_prompt.dtype)
        mix, mk, mv, wk, wv, cs, ss = mixer_prompt(hp, pos, valid, w_in[l], attn_sinks[l], conv_w[l], conv_b[l],
                                                  dt_bias[l], a_log[l], d_skip[l], ssm_norm_w[l], w_o[l])
        hp = layer_norm(ALPHA * hp + mix, ln1_g[l], ln1_b[l])
        hp = layer_norm(ALPHA * hp + hier_moe(hp, router_group_w[l], router_group_b[l], router_expert_w[l],
                                              router_expert_b[l], w_gate[l], w_up[l], w_down[l]), ln2_g[l], ln2_b[l])
        mix_s, nk, nv, ncs, nss = mixer_sample(hs, cache_meta_k[l], cache_meta_v[l], cache_win_k[l], cache_win_v[l],
                                               state_conv[l], state_ssm[l], w_in[l], attn_sinks[l], conv_w[l],
                                               conv_b[l], dt_bias[l], a_log[l], d_skip[l], ssm_norm_w[l], w_o[l])
        hs = layer_norm(ALPHA * hs + mix_s, ln1_g[l], ln1_b[l])
        hs = layer_norm(ALPHA * hs + hier_moe(hs, router_group_w[l], router_group_b[l], router_expert_w[l],
                                              router_expert_b[l], w_gate[l], w_up[l], w_down[l]), ln2_g[l], ln2_b[l])
        pmk.append(mk); pmv.append(mv); pwk.append(wk); pwv.append(wv); pcv.append(cs); pss.append(ss)
        swk.append(nk); swv.append(nv); scv.append(ncs); sss.append(nss)
    y_prompt = hp[:, BLOCK:]
    y_sample = hs
    return (y_prompt, y_sample,
            jnp.stack(pmk), jnp.stack(pmv), jnp.stack(pwk), jnp.stack(pwv), jnp.stack(pcv), jnp.stack(pss),
            jnp.stack(swk), jnp.stack(swv), jnp.stack(scv), jnp.stack(sss))
```

```python
import functools
import math

import jax
import jax.numpy as jnp
from jax import lax
from jax.experimental import pallas as pl
from jax.experimental.pallas import tpu as pltpu

F32 = jnp.float32
BF16 = jnp.bfloat16

D_MODEL = 1024
N_META = 16
N_HEADS = 16
N_KV_HEADS = 4
HEAD_DIM = 64
HALF = HEAD_DIM // 2
Q_PER_KV = N_HEADS // N_KV_HEADS
WINDOW = 128
BLOCK = 128
META_PAD = BLOCK - N_META
ROPE_THETA = 10000.0
PAST_LEN = 16384
SSM_INNER = D_MODEL
SSM_HEAD_DIM = 64
SSM_HEADS = SSM_INNER // SSM_HEAD_DIM
SSM_GROUPS = 4
SSM_STATE = 128
CONV_WIDTH = 4
GN = SSM_GROUPS * SSM_STATE
CONV_DIM = SSM_INNER + 2 * GN
Q_DIM = N_HEADS * HEAD_DIM
KV_DIM = N_KV_HEADS * HEAD_DIM
N_EXPERT_GROUPS = 4
EXPERTS_PER_GROUP = 8
N_EXPERTS = N_EXPERT_GROUPS * EXPERTS_PER_GROUP
D_EXPERT = 256
DEPTH = 1
ALPHA = (2 * DEPTH) ** 0.25
LN_EPS = 1e-5
RMS_EPS = 1e-5
NEG_INF = -1e30

LANES = 128
SUBLANES = 8
VMEM_LIMIT = 56 * 1024 * 1024

C_Q = 0
C_K = C_Q + Q_DIM
C_V = C_K + KV_DIM
C_Z = C_V + KV_DIM
C_XBC = C_Z + SSM_INNER
C_GA = C_XBC + CONV_DIM
C_GS = C_GA + D_MODEL
C_DT = C_GS + D_MODEL
IN_PAD = C_DT + LANES
ROUTE_OFF = N_EXPERT_GROUPS


def _const_spec(shape):
    nd = len(shape)
    return pl.BlockSpec(shape, lambda *_: (0,) * nd, pipeline_mode=pl.Buffered(1))


def _dot(a, b):
    return jnp.dot(a, b, preferred_element_type=F32)


def _dot_nt(a, b):
    return lax.dot_general(a, b, (((1,), (1,)), ((), ())), preferred_element_type=F32)


def _dot_tn(a, b):
    return lax.dot_general(a, b, (((0,), (0,)), ((), ())), preferred_element_type=F32)


def _split3(x):
    p1 = x.astype(BF16)
    r1 = x - p1.astype(F32)
    p2 = r1.astype(BF16)
    r2 = r1 - p2.astype(F32)
    return p1, p2, r2.astype(BF16)


def _dot_exact_rhs(x, m_bf16):
    p1, p2, p3 = _split3(x)
    return _dot(p1, m_bf16) + _dot(p2, m_bf16) + _dot(p3, m_bf16)


def _dot_exact_lhs(m_bf16, x):
    p1, p2, p3 = _split3(x)
    return _dot(m_bf16, p1) + _dot(m_bf16, p2) + _dot(m_bf16, p3)


def _silu(x):
    return x * jax.nn.sigmoid(x)


def _softplus(x):
    return jnp.maximum(x, 0.0) + jnp.log1p(jnp.exp(-jnp.abs(x)))


def _layer_norm(x, g, b):
    mu = jnp.mean(x, -1, keepdims=True)
    xc = x - mu
    var = jnp.mean(xc * xc, -1, keepdims=True)
    return xc * lax.rsqrt(var + LN_EPS) * g + b


def _rope_tiles(t, cos, sin_signed):
    lane = lax.broadcasted_iota(jnp.int32, (t.shape[0], LANES), 1)
    first_half = (lane % HEAD_DIM) < HALF
    outs = []
    for i in range(t.shape[1] // LANES):
        x = t[:, i * LANES:(i + 1) * LANES]
        partner = jnp.where(first_half, pltpu.roll(x, LANES - HALF, 1), pltpu.roll(x, HALF, 1))
        outs.append(x * cos + partner * sin_signed)
    return jnp.concatenate(outs, axis=1) if len(outs) > 1 else outs[0]


def _gated_rmsnorm(y, z, nw):
    yf = y * _silu(z)
    gw = SSM_INNER // SSM_GROUPS
    outs = []
    for g in range(SSM_GROUPS):
        t = yf[:, g * gw:(g + 1) * gw]
        ms = jnp.mean(t * t, -1, keepdims=True)
        outs.append(t * lax.rsqrt(ms + RMS_EPS))
    return jnp.concatenate(outs, axis=1) * nw


def _router_logits(h, wr_hi, wr_lo, br):
    h_hi = h.astype(BF16)
    h_lo = (h - h_hi.astype(F32)).astype(BF16)
    return _dot(h_hi, wr_hi) + _dot(h_lo, wr_hi) + _dot(h_hi, wr_lo) + br


def _prompt_kernel(sinks_ref, x_ref, meta_ref, win_ref, wo_ref, cos_ref, sin_ref, convw_ref, convb_ref,
                   dtb_ref, alog_ref, dskip_ref, nw_ref, g1_ref, b1_ref, wrh_ref, wrl_ref, br_ref,
                   h1_ref, rl_ref, kmeta_ref, vmeta_ref, kwin_ref, vwin_ref, conv_ref, ssm_ref,
                   kd_sc, vd_sc, xbuf_sc, st_sc):
    j = pl.program_id(1)
    last = pl.num_programs(1) - 1
    T = BLOCK

    @pl.when(j == 0)
    def _():
        kd_sc[...] = jnp.zeros_like(kd_sc)
        vd_sc[...] = jnp.zeros_like(vd_sc)
        xbuf_sc[0:SUBLANES, :] = jnp.zeros((SUBLANES, CONV_DIM), F32)
        st_sc[...] = jnp.zeros_like(st_sc)

    x = jnp.where(j == 0, meta_ref[...], x_ref[...])
    xb = x.astype(BF16)

    lane = lax.broadcasted_iota(jnp.int32, (T, LANES), 1)
    row = lax.broadcasted_iota(jnp.int32, (T, LANES), 0)
    lo = lane < HEAD_DIM

    cos = cos_ref[...]
    sin = sin_ref[...]

    q = _rope_tiles(_dot(xb, win_ref[:, C_Q:C_K]), cos, sin)
    qb = (q * (HEAD_DIM ** -0.5)).astype(BF16)
    k = _rope_tiles(_dot(xb, win_ref[:, C_K:C_V]), cos, sin)
    v = _dot(xb, win_ref[:, C_V:C_Z])

    @pl.when(j == 0)
    def _():
        kmeta_ref[...] = k[META_PAD:, :]
        vmeta_ref[...] = v[META_PAD:, :]

    @pl.when(j == last)
    def _():
        kwin_ref[...] = k
        vwin_ref[...] = v

    def dup(t, g):
        tile = t[:, (g // 2) * LANES:(g // 2 + 1) * LANES]
        rolled = pltpu.roll(tile, HEAD_DIM, 1)
        return jnp.where(lo if g % 2 == 0 else jnp.logical_not(lo), tile, rolled)

    kd_cur = jnp.concatenate([dup(k, g) for g in range(N_KV_HEADS)], axis=1).astype(BF16)
    vd_cur = jnp.concatenate([dup(v, g) for g in range(N_KV_HEADS)], axis=1).astype(BF16)
    kd_sc[2 * T:3 * T, :] = kd_cur
    vd_sc[2 * T:3 * T, :] = vd_cur

    @pl.when(j == 0)
    def _():
        kd_sc[0:T, :] = kd_cur
        vd_sc[0:T, :] = vd_cur

    nq = Q_PER_KV * T
    r4 = lax.broadcasted_iota(jnp.int32, (nq, 3 * T), 0)
    c4 = lax.broadcasted_iota(jnp.int32, (nq, 3 * T), 1)
    rr = r4 % T
    ss = c4 % T
    kb = c4 // T
    mask = (((kb == 0) & (ss >= META_PAD) & ((j >= 1) | (ss <= rr)))
            | ((kb == 1) & (ss > rr) & (j >= 2))
            | ((kb == 2) & (ss <= rr) & (j >= 1)))
    rcol = lax.broadcasted_iota(jnp.int32, (nq, 1), 0)

    attn_tiles = []
    for g in range(N_KV_HEADS):
        qt0 = qb[:, (2 * g) * LANES:(2 * g + 1) * LANES]
        qt1 = qb[:, (2 * g + 1) * LANES:(2 * g + 2) * LANES]
        zero = jnp.zeros_like(qt0)
        lhs = jnp.concatenate([jnp.where(lo, qt0, zero), jnp.where(lo, zero, qt0),
                               jnp.where(lo, qt1, zero), jnp.where(lo, zero, qt1)], axis=0)
        s = _dot_nt(lhs, kd_sc[:, g * LANES:(g + 1) * LANES])
        s = jnp.where(mask, s, NEG_INF)
        sink = jnp.where(rcol < T, sinks_ref[4 * g],
                         jnp.where(rcol < 2 * T, sinks_ref[4 * g + 1],
                                   jnp.where(rcol < 3 * T, sinks_ref[4 * g + 2], sinks_ref[4 * g + 3])))
        m = jnp.maximum(jnp.max(s, axis=1, keepdims=True), sink)
        p = jnp.exp(s - m)
        den = jnp.sum(p, axis=1, keepdims=True) + jnp.exp(sink - m)
        o = _dot(p.astype(BF16), vd_sc[:, g * LANES:(g + 1) * LANES]) / den
        attn_tiles.append(jnp.where(lo, o[0:T], o[T:2 * T]))
        attn_tiles.append(jnp.where(lo, o[2 * T:3 * T], o[3 * T:4 * T]))
    y_attn = jnp.concatenate(attn_tiles, axis=1)

    kd_sc[T:2 * T, :] = kd_cur
    vd_sc[T:2 * T, :] = vd_cur

    z = _dot(xb, win_ref[:, C_Z:C_XBC])
    xbc_raw = _dot(xb, win_ref[:, C_XBC:C_GA])
    xbuf_sc[SUBLANES:SUBLANES + T, :] = xbc_raw
    acc = convb_ref[...]
    for kk in range(CONV_WIDTH):
        acc = acc + xbuf_sc[SUBLANES - kk:SUBLANES - kk + T, :] * convw_ref[CONV_WIDTH - 1 - kk:CONV_WIDTH - kk, :]

    @pl.when(j == last)
    def _():
        conv_ref[...] = xbuf_sc[SUBLANES + T - (CONV_WIDTH - 1):SUBLANES + T, :]

    xbuf_sc[0:SUBLANES, :] = xbuf_sc[T:T + SUBLANES, :]
    xc = _silu(acc)
    xs = xc[:, :SSM_INNER]
    bm = xc[:, SSM_INNER:SSM_INNER + GN].astype(BF16)
    cm = xc[:, SSM_INNER + GN:].astype(BF16)

    valid = (j * T + row - META_PAD) >= 0
    dt = jnp.where(valid, _softplus(_dot(xb, win_ref[:, C_DT:IN_PAD]) + dtb_ref[...]), 0.0)
    ad = dt * (-jnp.exp(alog_ref[...]))
    tril = row >= lane
    cs = _dot_exact_lhs(tril.astype(BF16), ad)
    cs_t = cs.T

    hp_row = lax.broadcasted_iota(jnp.int32, (LANES, LANES), 0) < SSM_HEAD_DIM
    y_tiles = []
    for i in range(SSM_HEADS // 2):
        g = i // 2
        h0, h1 = 2 * i, 2 * i + 1
        bg = bm[:, g * SSM_STATE:(g + 1) * SSM_STATE]
        cg = cm[:, g * SSM_STATE:(g + 1) * SSM_STATE]
        cb = _dot_nt(cg, bg)
        col0, col1 = cs[:, h0:h0 + 1], cs[:, h1:h1 + 1]
        row0, row1 = cs_t[h0:h0 + 1, :], cs_t[h1:h1 + 1, :]
        tot0, tot1 = cs_t[h0:h0 + 1, T - 1:T], cs_t[h1:h1 + 1, T - 1:T]
        m0 = (cb * jnp.exp(jnp.where(tril, col0 - row0, NEG_INF))).astype(BF16)
        m1 = (cb * jnp.exp(jnp.where(tril, col1 - row1, NEG_INF))).astype(BF16)
        xs_i = xs[:, i * LANES:(i + 1) * LANES]
        xd = xs_i * jnp.where(lo, dt[:, h0:h0 + 1], dt[:, h1:h1 + 1])
        xdb = xd.astype(BF16)
        y_diag = jnp.where(lo, _dot(m0, xdb), _dot(m1, xdb))
        s_pair = st_sc[i * LANES:(i + 1) * LANES, :]
        y_off = _dot_nt(cg, s_pair.astype(BF16)) * jnp.where(lo, jnp.exp(col0), jnp.exp(col1))
        wgt = jnp.where(lo, jnp.exp(tot0 - col0), jnp.exp(tot1 - col1))
        upd = _dot_tn((xd * wgt).astype(BF16), bg)
        st_sc[i * LANES:(i + 1) * LANES, :] = jnp.where(hp_row, jnp.exp(tot0), jnp.exp(tot1)) * s_pair + upd
        y_tiles.append(y_diag + y_off + dskip_ref[:, i * LANES:(i + 1) * LANES] * xs_i)
    y = jnp.concatenate(y_tiles, axis=1)

    @pl.when(j == last)
    def _():
        ssm_ref[...] = st_sc[...]

    y_ssd = _gated_rmsnorm(y, z, nw_ref[...])

    ga = _dot(xb, win_ref[:, C_GA:C_GS])
    gs = _dot(xb, win_ref[:, C_GS:C_DT])
    mixin = jax.nn.sigmoid(ga) * y_attn + jax.nn.sigmoid(gs) * y_ssd
    mix = _dot(mixin.astype(BF16), wo_ref[...])
    h1 = _layer_norm(ALPHA * x + mix, g1_ref[...], b1_ref[...])
    h1_ref[...] = h1
    rl_ref[...] = _router_logits(h1, wrh_ref[...], wrl_ref[...], br_ref[...])


def _prompt_call(x_prompt, meta_blk, w_in_r, w_o_b, cos_t, sin_t, conv_w, conv_b, dtb, alog, dskip_e, nw,
                 g1, b1, wr_hi, wr_lo, br, sinks):
    B, S, D = x_prompt.shape
    nb = S // BLOCK + 1
    tok = lambda b, j, *_: (b, jnp.maximum(j - 1, 0), 0)
    perb = lambda b, j, *_: (b, 0, 0)
    grid_spec = pltpu.PrefetchScalarGridSpec(
        num_scalar_prefetch=1,
        grid=(B, nb),
        in_specs=[
            pl.BlockSpec((None, BLOCK, D), tok),
            _const_spec((BLOCK, D)),
            _const_spec((D, IN_PAD)),
            _const_spec((D, D)),
            pl.BlockSpec((BLOCK, LANES), lambda b, j, *_: (j, 0)),
            pl.BlockSpec((BLOCK, LANES), lambda b, j, *_: (j, 0)),
            _const_spec((CONV_WIDTH, CONV_DIM)),
            _const_spec((1, CONV_DIM)),
            _const_spec((1, LANES)),
            _const_spec((1, LANES)),
            _const_spec((1, SSM_INNER)),
            _const_spec((1, SSM_INNER)),
            _const_spec((1, D)),
            _const_spec((1, D)),
            _const_spec((D, LANES)),
            _const_spec((D, LANES)),
            _const_spec((1, LANES)),
        ],
        out_specs=[
            pl.BlockSpec((None, BLOCK, D), tok),
            pl.BlockSpec((None, BLOCK, LANES), tok),
            pl.BlockSpec((None, N_META, KV_DIM), perb),
            pl.BlockSpec((None, N_META, KV_DIM), perb),
            pl.BlockSpec((None, WINDOW, KV_DIM), perb),
            pl.BlockSpec((None, WINDOW, KV_DIM), perb),
            pl.BlockSpec((None, CONV_WIDTH - 1, CONV_DIM), perb),
            pl.BlockSpec((None, SSM_INNER, SSM_STATE), perb),
        ],
        scratch_shapes=[
            pltpu.VMEM((3 * BLOCK, N_KV_HEADS * LANES), BF16),
            pltpu.VMEM((3 * BLOCK, N_KV_HEADS * LANES), BF16),
            pltpu.VMEM((SUBLANES + BLOCK, CONV_DIM), F32),
            pltpu.VMEM((SSM_INNER, SSM_STATE), F32),
        ],
    )
    out_shape = [
        jax.ShapeDtypeStruct((B, S, D), F32),
        jax.ShapeDtypeStruct((B, S, LANES), F32),
        jax.ShapeDtypeStruct((B, N_META, KV_DIM), F32),
        jax.ShapeDtypeStruct((B, N_META, KV_DIM), F32),
        jax.ShapeDtypeStruct((B, WINDOW, KV_DIM), F32),
        jax.ShapeDtypeStruct((B, WINDOW, KV_DIM), F32),
        jax.ShapeDtypeStruct((B, CONV_WIDTH - 1, CONV_DIM), F32),
        jax.ShapeDtypeStruct((B, SSM_INNER, SSM_STATE), F32),
    ]
    return pl.pallas_call(
        _prompt_kernel, grid_spec=grid_spec, out_shape=out_shape, name="prompt_mixer",
        compiler_params=pltpu.CompilerParams(dimension_semantics=("arbitrary", "arbitrary"),
                                             vmem_limit_bytes=VMEM_LIMIT),
    )(sinks, x_prompt, meta_blk, w_in_r, w_o_b, cos_t, sin_t, conv_w, conv_b, dtb, alog, dskip_e, nw,
      g1, b1, wr_hi, wr_lo, br)


def _sample_proj_kernel(x_ref, win_ref, cos_ref, sin_ref, h0_ref, h1_ref, h2_ref, convw_ref, convb_ref,
                        dtb_ref, alog_ref, expand_ref,
                        q_ref, k_ref, v_ref, z_ref, xraw_ref, xdt_ref, dec_ref, xs_ref, b_ref, c_ref, ga_ref, gs_ref):
    xb = x_ref[...].astype(BF16)
    cos = cos_ref[...]
    sin = sin_ref[...]
    q_ref[...] = _rope_tiles(_dot(xb, win_ref[:, C_Q:C_K]), cos, sin) * (HEAD_DIM ** -0.5)
    k_ref[...] = _rope_tiles(_dot(xb, win_ref[:, C_K:C_V]), cos, sin)
    v_ref[...] = _dot(xb, win_ref[:, C_V:C_Z])
    z_ref[...] = _dot(xb, win_ref[:, C_Z:C_XBC])
    xraw = _dot(xb, win_ref[:, C_XBC:C_GA])
    xraw_ref[...] = xraw
    acc = (convb_ref[...] + h0_ref[...] * convw_ref[0:1, :] + h1_ref[...] * convw_ref[1:2, :]
           + h2_ref[...] * convw_ref[2:3, :] + xraw * convw_ref[3:4, :])
    xc = _silu(acc)
    xs = xc[:, :SSM_INNER]
    dt = _softplus(_dot(xb, win_ref[:, C_DT:IN_PAD]) + dtb_ref[...])
    ad = dt * (-jnp.exp(alog_ref[...]))
    expand = expand_ref[...]
    dt_e = _dot_exact_rhs(dt, expand)
    ad_e = _dot_exact_rhs(ad, expand)
    xs_ref[...] = xs
    xdt_ref[...] = xs * dt_e
    dec_ref[...] = jnp.exp(ad_e)
    b_ref[...] = xc[:, SSM_INNER:SSM_INNER + GN]
    c_ref[...] = xc[:, SSM_INNER + GN:]
    ga_ref[...] = _dot(xb, win_ref[:, C_GA:C_GS])
    gs_ref[...] = _dot(xb, win_ref[:, C_GS:C_DT])


def _sample_proj_call(xs2d, w_in_r, cos_s, sin_s, hist0, hist1, hist2, conv_w, conv_b, dtb, alog, expand):
    n = xs2d.shape[0]
    shapes = [(n, Q_DIM), (n, KV_DIM), (n, KV_DIM), (n, SSM_INNER), (n, CONV_DIM), (n, SSM_INNER), (n, SSM_INNER),
              (n, SSM_INNER), (n, GN), (n, GN), (n, D_MODEL), (n, D_MODEL)]
    return pl.pallas_call(
        _sample_proj_kernel,
        out_shape=[jax.ShapeDtypeStruct(s, F32) for s in shapes], name="sample_proj",
        compiler_params=pltpu.CompilerParams(vmem_limit_bytes=VMEM_LIMIT),
    )(xs2d, w_in_r, cos_s, sin_s, hist0, hist1, hist2, conv_w, conv_b, dtb, alog, expand)


def _sample_attn_kernel(qblk_ref, mk_ref, mv_ref, wk_ref, wv_ref, nk_ref, nv_ref, sink_ref, o_ref):
    qb = qblk_ref[...]
    qbb = qb.astype(BF16)
    bdot = lambda a, b, dims: lax.dot_general(a, b, dims, preferred_element_type=F32)
    nt = (((2,), (2,)), ((0,), (0,)))
    nn = (((2,), (1,)), ((0,), (0,)))
    wk = wk_ref[...].astype(BF16)
    wv = wv_ref[...].astype(BF16)
    mk = mk_ref[...].astype(BF16)
    mv = mv_ref[...].astype(BF16)
    nk = nk_ref[...]
    nv = nv_ref[...]
    s_win = bdot(qbb, wk, nt)
    s_meta = bdot(qbb, mk, nt)
    s_new = jnp.sum(qb * nk, axis=2, keepdims=True)
    w = wk_ref.shape[1]
    slot = lax.broadcasted_iota(jnp.int32, s_win.shape, 2)
    s_win = jnp.where((w - slot) < WINDOW, s_win, NEG_INF)
    sink = sink_ref[...]
    m = jnp.maximum(jnp.maximum(jnp.max(s_win, axis=2, keepdims=True), jnp.max(s_meta, axis=2, keepdims=True)),
                    jnp.maximum(s_new, sink))
    p_win = jnp.exp(s_win - m)
    p_meta = jnp.exp(s_meta - m)
    p_new = jnp.exp(s_new - m)
    den = (jnp.sum(p_win, axis=2, keepdims=True) + jnp.sum(p_meta, axis=2, keepdims=True) + p_new
           + jnp.exp(sink - m))
    o = (bdot(p_win.astype(BF16), wv, nn) + bdot(p_meta.astype(BF16), mv, nn)
         + p_new * nv)
    o_ref[...] = o / den


def _sample_attn_call(qblk, mk, mv, wk, wv, nk, nv, sink3, nblk):
    n = qblk.shape[0]
    w = wk.shape[1]
    b3 = lambda shape: pl.BlockSpec(shape, lambda i: (i, 0, 0))
    return pl.pallas_call(
        _sample_attn_kernel,
        grid=(n // nblk,),
        in_specs=[b3((nblk, N_HEADS, KV_DIM)), b3((nblk, N_META, KV_DIM)), b3((nblk, N_META, KV_DIM)),
                  b3((nblk, w, KV_DIM)), b3((nblk, w, KV_DIM)), b3((nblk, 1, KV_DIM)), b3((nblk, 1, KV_DIM)),
                  pl.BlockSpec((1, N_HEADS, 1), lambda i: (0, 0, 0))],
        out_specs=b3((nblk, N_HEADS, KV_DIM)),
        out_shape=jax.ShapeDtypeStruct((n, N_HEADS, KV_DIM), F32), name="sample_attn",
        compiler_params=pltpu.CompilerParams(dimension_semantics=("arbitrary",), vmem_limit_bytes=VMEM_LIMIT),
    )(qblk, mk, mv, wk, wv, nk, nv, sink3)


def _sample_ssd_kernel(xdt_ref, dec_ref, b_ref, c_ref, st_ref, yattn_ref, xs_ref, z_ref, ga_ref, gs_ref, x_ref,
                       dskip_ref, nw_ref, wo_ref, g1_ref, b1_ref, wrh_ref, wrl_ref, br_ref,
                       st_out_ref, h1_ref, rl_ref, xdt_t_sc, dec_t_sc, y_t_sc, *, nblk):
    i = pl.program_id(0)
    n_all = xdt_ref.shape[0]

    @pl.when(i == 0)
    def _():
        xdt_t_sc[...] = xdt_ref[...].T
        dec_t_sc[...] = dec_ref[...].T
        y_t_sc[...] = jnp.zeros_like(y_t_sc)

    r_id = lax.broadcasted_iota(jnp.int32, (n_all, LANES), 0)
    c_id = lax.broadcasted_iota(jnp.int32, (n_all, LANES), 1)
    for t in range(nblk):
        n = i * nblk + t
        pick_row = (r_id == n).astype(BF16)
        pick_col = (c_id == n).astype(BF16)
        xb_n = _dot_exact_rhs(xdt_t_sc[...], pick_row)
        db_n = _dot_exact_rhs(dec_t_sc[...], pick_row)
        brow = b_ref[pl.ds(n, 1), :]
        crow = c_ref[pl.ds(n, 1), :]
        gw = SSM_INNER // SSM_GROUPS
        bfull = jnp.concatenate([jnp.broadcast_to(brow[:, g * SSM_STATE:(g + 1) * SSM_STATE], (gw, SSM_STATE))
                                 for g in range(SSM_GROUPS)], axis=0)
        cfull = jnp.concatenate([jnp.broadcast_to(crow[:, g * SSM_STATE:(g + 1) * SSM_STATE], (gw, SSM_STATE))
                                 for g in range(SSM_GROUPS)], axis=0)
        h_new = st_ref[t] * db_n + xb_n * bfull
        st_out_ref[t] = h_new
        y_t_sc[...] += _dot_exact_rhs(h_new * cfull, pick_col)

    @pl.when(i == pl.num_programs(0) - 1)
    def _():
        xs = xs_ref[...]
        y = y_t_sc[...].T + dskip_ref[...] * xs
        y_ssd = _gated_rmsnorm(y, z_ref[...], nw_ref[...])
        mixin = jax.nn.sigmoid(ga_ref[...]) * yattn_ref[...] + jax.nn.sigmoid(gs_ref[...]) * y_ssd
        mix = _dot(mixin.astype(BF16), wo_ref[...])
        h1 = _layer_norm(ALPHA * x_ref[...] + mix, g1_ref[...], b1_ref[...])
        h1_ref[...] = h1
        rl_ref[...] = _router_logits(h1, wrh_ref[...], wrl_ref[...], br_ref[...])


def _sample_ssd_call(xdt, dec, bmat, cmat, state3, y_attn, xs, z, ga, gs, x2d, dskip_e, nw, w_o_b, g1, b1,
                     wr_hi, wr_lo, br, nblk):
    n = xdt.shape[0]
    full = lambda a: pl.BlockSpec(a.shape, lambda i: (0,) * a.ndim)
    st_spec = pl.BlockSpec((nblk, SSM_INNER, SSM_STATE), lambda i: (i, 0, 0))
    ins = [xdt, dec, bmat, cmat, state3, y_attn, xs, z, ga, gs, x2d, dskip_e, nw, w_o_b, g1, b1, wr_hi, wr_lo, br]
    in_specs = [full(a) for a in ins]
    in_specs[4] = st_spec
    return pl.pallas_call(
        functools.partial(_sample_ssd_kernel, nblk=nblk),
        grid=(n // nblk,),
        in_specs=in_specs,
        out_specs=[st_spec, pl.BlockSpec((n, D_MODEL), lambda i: (0, 0)), pl.BlockSpec((n, LANES), lambda i: (0, 0))],
        out_shape=[jax.ShapeDtypeStruct(state3.shape, F32), jax.ShapeDtypeStruct((n, D_MODEL), F32),
                   jax.ShapeDtypeStruct((n, LANES), F32)],
        scratch_shapes=[pltpu.VMEM((SSM_INNER, n), F32), pltpu.VMEM((SSM_INNER, n), F32),
                        pltpu.VMEM((SSM_INNER, n), F32)],
        name="sample_ssd",
        compiler_params=pltpu.CompilerParams(dimension_semantics=("arbitrary",), vmem_limit_bytes=VMEM_LIMIT),
    )(*ins)


def _combine_weights(rl):
    lane = lax.broadcasted_iota(jnp.int32, rl.shape, 1)
    big = jnp.int32(1 << 20)
    is_g = lane < N_EXPERT_GROUPS
    gmax = jnp.max(jnp.where(is_g, rl, -jnp.inf), axis=1, keepdims=True)
    gsel = jnp.min(jnp.where(is_g & (rl == gmax), lane, big), axis=1, keepdims=True)
    gden = jnp.sum(jnp.where(is_g, jnp.exp(rl - gmax), 0.0), axis=1, keepdims=True)
    g_w = 1.0 / gden
    e_id = lane - ROUTE_OFF
    in_grp = (e_id >= 0) & (e_id < N_EXPERTS) & ((e_id // EXPERTS_PER_GROUP) == gsel)
    v1 = jnp.max(jnp.where(in_grp, rl, -jnp.inf), axis=1, keepdims=True)
    i1 = jnp.min(jnp.where(in_grp & (rl == v1), lane, big), axis=1, keepdims=True)
    rest = in_grp & (lane != i1)
    v2 = jnp.max(jnp.where(rest, rl, -jnp.inf), axis=1, keepdims=True)
    i2 = jnp.min(jnp.where(rest & (rl == v2), lane, big), axis=1, keepdims=True)
    e21 = jnp.exp(v2 - v1)
    p1 = g_w / (1.0 + e21)
    p2 = g_w * e21 / (1.0 + e21)
    return jnp.where(lane == i1, p1, 0.0) + jnp.where(lane == i2, p2, 0.0)


def _moe_dense_kernel(h_ref, rl_ref, wg_ref, wu_ref, wd_ref, g2_ref, b2_ref, o_ref, cw_sc, acc_sc):
    e = pl.program_id(1)

    @pl.when(e == 0)
    def _():
        cw_sc[...] = _combine_weights(rl_ref[...])
        acc_sc[...] = jnp.zeros_like(acc_sc)

    cw = cw_sc[...]
    lane = lax.broadcasted_iota(jnp.int32, cw.shape, 1)
    col = jnp.sum(jnp.where(lane == e + ROUTE_OFF, cw, 0.0), axis=1, keepdims=True)
    xb = h_ref[...].astype(BF16)
    hid = _silu(_dot(xb, wg_ref[...])) * _dot(xb, wu_ref[...]) * col
    acc_sc[...] += _dot(hid.astype(BF16), wd_ref[...])

    @pl.when(e == pl.num_programs(1) - 1)
    def _():
        o_ref[...] = _layer_norm(ALPHA * h_ref[...] + acc_sc[...], g2_ref[...], b2_ref[...])


def _moe_dense_call(h, rl, wg, wu, wd, g2, b2, tm):
    t = h.shape[0]
    return pl.pallas_call(
        _moe_dense_kernel,
        grid=(t // tm, N_EXPERTS),
        in_specs=[pl.BlockSpec((tm, D_MODEL), lambda i, e: (i, 0)),
                  pl.BlockSpec((tm, LANES), lambda i, e: (i, 0)),
                  pl.BlockSpec((None, D_MODEL, D_EXPERT), lambda i, e: (e, 0, 0)),
                  pl.BlockSpec((None, D_MODEL, D_EXPERT), lambda i, e: (e, 0, 0)),
                  pl.BlockSpec((None, D_EXPERT, D_MODEL), lambda i, e: (e, 0, 0)),
                  pl.BlockSpec((1, D_MODEL), lambda i, e: (0, 0)),
                  pl.BlockSpec((1, D_MODEL), lambda i, e: (0, 0))],
        out_specs=pl.BlockSpec((tm, D_MODEL), lambda i, e: (i, 0)),
        out_shape=jax.ShapeDtypeStruct((t, D_MODEL), F32),
        scratch_shapes=[pltpu.VMEM((tm, LANES), F32), pltpu.VMEM((tm, D_MODEL), F32)],
        name="moe_dense",
        compiler_params=pltpu.CompilerParams(dimension_semantics=("arbitrary", "arbitrary"),
                                             vmem_limit_bytes=VMEM_LIMIT),
    )(h, rl, wg, wu, wd, g2, b2)


def _rope_tables(pos):
    inv_freq = jnp.exp(-math.log(ROPE_THETA) * jnp.arange(HALF, dtype=F32) * (2.0 / HEAD_DIM))
    ang = pos.astype(F32)[:, None] * inv_freq[None, :]
    cos = jnp.cos(ang)
    sin = jnp.sin(ang)
    cos_t = jnp.tile(jnp.concatenate([cos, cos], axis=1), (1, LANES // HEAD_DIM))
    sin_t = jnp.tile(jnp.concatenate([-sin, sin], axis=1), (1, LANES // HEAD_DIM))
    return cos_t, sin_t


def _pad_lanes(a, n=LANES):
    return jnp.pad(a, ((0, 0), (0, n - a.shape[1])))


def kernel(x_prompt, x_sample, cache_meta_k, cache_meta_v, cache_win_k, cache_win_v, state_conv, state_ssm,
           meta_tokens, w_in, attn_sinks, conv_w, conv_b, dt_bias, a_log, d_skip, ssm_norm_w, w_o,
           ln1_g, ln1_b, router_group_w, router_group_b, router_expert_w, router_expert_b,
           w_gate, w_up, w_down, ln2_g, ln2_b):
    B, S, D = x_prompt.shape
    N = x_sample.shape[0]
    l = 0

    wi = w_in[l]
    dt0 = C_GA
    w_in_r = jnp.concatenate([wi[:, :dt0], wi[:, dt0 + SSM_HEADS:], wi[:, dt0:dt0 + SSM_HEADS],
                              jnp.zeros((D, LANES - SSM_HEADS), wi.dtype)], axis=1).astype(BF16)
    w_o_b = w_o[l].astype(BF16)
    wr = _pad_lanes(jnp.concatenate([router_group_w[l], router_expert_w[l]], axis=1))
    wr_hi = wr.astype(BF16)
    wr_lo = (wr - wr_hi.astype(F32)).astype(BF16)
    br = _pad_lanes(jnp.concatenate([router_group_b[l], router_expert_b[l]])[None, :])
    dtb = _pad_lanes(dt_bias[l][None, :])
    alog = _pad_lanes(a_log[l][None, :])
    dskip_e = jnp.repeat(d_skip[l], SSM_HEAD_DIM)[None, :]
    nw = ssm_norm_w[l][None, :]
    g1, b1 = ln1_g[l][None, :], ln1_b[l][None, :]
    g2, b2 = ln2_g[l][None, :], ln2_b[l][None, :]
    cw_, cb_ = conv_w[l], conv_b[l][None, :]
    sinks = attn_sinks[l]
    wg_b = w_gate[l].astype(BF16)
    wu_b = w_up[l].astype(BF16)
    wd_b = w_down[l].astype(BF16)

    nb = S // BLOCK + 1
    pos_p = jnp.arange(nb * BLOCK, dtype=jnp.int32) - META_PAD
    cos_p, sin_p = _rope_tables(pos_p)
    meta_blk = jnp.concatenate([jnp.zeros((META_PAD, D), F32), meta_tokens.astype(F32)], axis=0)
    (h1_p, rl_p, p_mk, p_mv, p_wk, p_wv, p_conv, p_ssm) = _prompt_call(
        x_prompt, meta_blk, w_in_r, w_o_b, cos_p, sin_p, cw_, cb_, dtb, alog, dskip_e, nw, g1, b1,
        wr_hi, wr_lo, br, sinks)
    tm_p = 1024 if (B * S) % 1024 == 0 else BLOCK
    y_prompt = _moe_dense_call(h1_p.reshape(B * S, D), rl_p.reshape(B * S, LANES), wg_b, wu_b, wd_b, g2, b2,
                               tm_p).reshape(B, S, D)

    xs2d = x_sample.reshape(N, D)
    cos_s, sin_s = _rope_tables(jnp.full((1,), PAST_LEN, jnp.int32))
    sc = state_conv[l]
    expand = (jnp.arange(LANES, dtype=jnp.int32)[:, None]
              == (jnp.arange(SSM_INNER, dtype=jnp.int32) // SSM_HEAD_DIM)[None, :]).astype(BF16)
    (q_s, k_s, v_s, z_s, xraw_s, xdt_s, dec_s, xsc_s, b_s, c_s, ga_s, gs_s) = _sample_proj_call(
        xs2d, w_in_r, cos_s, sin_s, sc[:, 0], sc[:, 1], sc[:, 2], cw_, cb_, dtb, alog, expand)
    head_group = jnp.arange(N_HEADS, dtype=jnp.int32) // Q_PER_KV
    grp_mask = (head_group[:, None] == jnp.arange(N_KV_HEADS, dtype=jnp.int32)[None, :]).astype(F32)
    qblk = (q_s.reshape(N, N_HEADS, 1, HEAD_DIM) * grp_mask[None, :, :, None]).reshape(N, N_HEADS, KV_DIM)
    wlen = cache_win_k.shape[2]
    o_blk = _sample_attn_call(
        qblk, cache_meta_k[l].reshape(N, N_META, KV_DIM), cache_meta_v[l].reshape(N, N_META, KV_DIM),
        cache_win_k[l].reshape(N, wlen, KV_DIM), cache_win_v[l].reshape(N, wlen, KV_DIM),
        k_s.reshape(N, 1, KV_DIM), v_s.reshape(N, 1, KV_DIM), sinks.reshape(1, N_HEADS, 1), 16)
    o4 = o_blk.reshape(N, N_HEADS, N_KV_HEADS, HEAD_DIM)
    y_attn_s = jnp.sum(o4 * grp_mask[None, :, :, None], axis=2).reshape(N, Q_DIM)
    s_ssm3, h1_s, rl_s = _sample_ssd_call(
        xdt_s, dec_s, b_s, c_s, state_ssm[l].reshape(N, SSM_INNER, SSM_STATE), y_attn_s, xsc_s, z_s, ga_s, gs_s,
        xs2d, dskip_e, nw, w_o_b, g1, b1, wr_hi, wr_lo, br, 8)
    y_sample = _moe_dense_call(h1_s, rl_s, wg_b, wu_b, wd_b, g2, b2, N).reshape(N, 1, D)

    kv5 = lambda a, n, t: a.reshape(1, n, t, N_KV_HEADS, HEAD_DIM)
    s_conv = jnp.stack([sc[:, 1], sc[:, 2], xraw_s], axis=1)[None]
    return (y_prompt, y_sample,
            kv5(p_mk, B, N_META), kv5(p_mv, B, N_META), kv5(p_wk, B, WINDOW), kv5(p_wv, B, WINDOW),
            p_conv[None], p_ssm.reshape(1, B, SSM_HEADS, SSM_HEAD_DIM, SSM_STATE),
            kv5(k_s, N, 1), kv5(v_s, N, 1), s_conv,
            s_ssm3.reshape(1, N, SSM_HEADS, SSM_HEAD_DIM, SSM_STATE))
```

```python
import functools
import math

import jax
import jax.numpy as jnp
from jax import lax
from jax.experimental import pallas as pl
from jax.experimental.pallas import tpu as pltpu

F32 = jnp.float32
BF16 = jnp.bfloat16

D_MODEL = 1024
N_META = 16
N_HEADS = 16
N_KV_HEADS = 4
HEAD_DIM = 64
HALF = HEAD_DIM // 2
Q_PER_KV = N_HEADS // N_KV_HEADS
WINDOW = 128
BLOCK = 128
META_PAD = BLOCK - N_META
ROPE_THETA = 10000.0
PAST_LEN = 16384
SSM_INNER = D_MODEL
SSM_HEAD_DIM = 64
SSM_HEADS = SSM_INNER // SSM_HEAD_DIM
SSM_GROUPS = 4
SSM_STATE = 128
CONV_WIDTH = 4
GN = SSM_GROUPS * SSM_STATE
CONV_DIM = SSM_INNER + 2 * GN
Q_DIM = N_HEADS * HEAD_DIM
KV_DIM = N_KV_HEADS * HEAD_DIM
N_EXPERT_GROUPS = 4
EXPERTS_PER_GROUP = 8
N_EXPERTS = N_EXPERT_GROUPS * EXPERTS_PER_GROUP
D_EXPERT = 256
DEPTH = 1
ALPHA = (2 * DEPTH) ** 0.25
LN_EPS = 1e-5
RMS_EPS = 1e-5
NEG_INF = -1e30

LANES = 128
SUBLANES = 8
VMEM_LIMIT = 56 * 1024 * 1024

C_Q = 0
C_K = C_Q + Q_DIM
C_V = C_K + KV_DIM
C_Z = C_V + KV_DIM
C_XBC = C_Z + SSM_INNER
C_GA = C_XBC + CONV_DIM
C_GS = C_GA + D_MODEL
C_DT = C_GS + D_MODEL
IN_PAD = C_DT + LANES
ROUTE_OFF = N_EXPERT_GROUPS
D_TILES = D_MODEL // LANES
TOKEN_BLOCK = 512
EXPERT_ROWS = 256
TOP_K = 2


def _to_tiles(ref, x, rows=None):
    for c in range(D_TILES):
        if rows is None:
            ref[:, c, :] = x[:, c * LANES:(c + 1) * LANES]
        else:
            ref[0:rows, c, :] = x[:, c * LANES:(c + 1) * LANES]


def _from_tiles(ref_or_view):
    return jnp.concatenate([ref_or_view[:, c, :] for c in range(D_TILES)], axis=1)


def _const_spec(shape):
    nd = len(shape)
    return pl.BlockSpec(shape, lambda *_: (0,) * nd, pipeline_mode=pl.Buffered(1))


def _dot(a, b):
    return jnp.dot(a, b, preferred_element_type=F32)


def _dot_nt(a, b):
    return lax.dot_general(a, b, (((1,), (1,)), ((), ())), preferred_element_type=F32)


def _dot_tn(a, b):
    return lax.dot_general(a, b, (((0,), (0,)), ((), ())), preferred_element_type=F32)


def _split3(x):
    p1 = x.astype(BF16)
    r1 = x - p1.astype(F32)
    p2 = r1.astype(BF16)
    r2 = r1 - p2.astype(F32)
    return p1, p2, r2.astype(BF16)


def _dot_exact_rhs(x, m_bf16):
    p1, p2, p3 = _split3(x)
    return _dot(p1, m_bf16) + _dot(p2, m_bf16) + _dot(p3, m_bf16)


def _dot_exact_lhs(m_bf16, x):
    p1, p2, p3 = _split3(x)
    return _dot(m_bf16, p1) + _dot(m_bf16, p2) + _dot(m_bf16, p3)


def _silu(x):
    return x * jax.nn.sigmoid(x)


def _softplus(x):
    return jnp.maximum(x, 0.0) + jnp.log1p(jnp.exp(-jnp.abs(x)))


def _layer_norm(x, g, b):
    mu = jnp.mean(x, -1, keepdims=True)
    xc = x - mu
    var = jnp.mean(xc * xc, -1, keepdims=True)
    return xc * lax.rsqrt(var + LN_EPS) * g + b


def _rope_tiles(t, cos, sin_signed):
    lane = lax.broadcasted_iota(jnp.int32, (t.shape[0], LANES), 1)
    first_half = (lane % HEAD_DIM) < HALF
    outs = []
    for i in range(t.shape[1] // LANES):
        x = t[:, i * LANES:(i + 1) * LANES]
        partner = jnp.where(first_half, pltpu.roll(x, LANES - HALF, 1), pltpu.roll(x, HALF, 1))
        outs.append(x * cos + partner * sin_signed)
    return jnp.concatenate(outs, axis=1) if len(outs) > 1 else outs[0]


def _gated_rmsnorm(y, z, nw):
    yf = y * _silu(z)
    gw = SSM_INNER // SSM_GROUPS
    outs = []
    for g in range(SSM_GROUPS):
        t = yf[:, g * gw:(g + 1) * gw]
        ms = jnp.mean(t * t, -1, keepdims=True)
        outs.append(t * lax.rsqrt(ms + RMS_EPS))
    return jnp.concatenate(outs, axis=1) * nw


def _router_logits(h, wr_hi, wr_lo, br):
    h_hi = h.astype(BF16)
    h_lo = (h - h_hi.astype(F32)).astype(BF16)
    return _dot(h_hi, wr_hi) + _dot(h_lo, wr_hi) + _dot(h_hi, wr_lo) + br


def _prompt_kernel(sinks_ref, x_ref, meta_ref, win_ref, wo_ref, cos_ref, sin_ref, convw_ref, convb_ref,
                   dtb_ref, alog_ref, dskip_ref, nw_ref, g1_ref, b1_ref, wrh_ref, wrl_ref, br_ref,
                   h1_ref, rl_ref, kmeta_ref, vmeta_ref, kwin_ref, vwin_ref, conv_ref, ssm_ref,
                   kd_sc, vd_sc, xbuf_sc, st_sc):
    j = pl.program_id(1)
    last = pl.num_programs(1) - 1
    T = BLOCK

    @pl.when(j == 0)
    def _():
        kd_sc[...] = jnp.zeros_like(kd_sc)
        vd_sc[...] = jnp.zeros_like(vd_sc)
        xbuf_sc[0:SUBLANES, :] = jnp.zeros((SUBLANES, CONV_DIM), F32)
        st_sc[...] = jnp.zeros_like(st_sc)

    x = jnp.where(j == 0, meta_ref[...], x_ref[...])
    xb = x.astype(BF16)

    lane = lax.broadcasted_iota(jnp.int32, (T, LANES), 1)
    row = lax.broadcasted_iota(jnp.int32, (T, LANES), 0)
    lo = lane < HEAD_DIM

    cos = cos_ref[...]
    sin = sin_ref[...]

    q = _rope_tiles(_dot(xb, win_ref[:, C_Q:C_K]), cos, sin)
    qb = (q * (HEAD_DIM ** -0.5)).astype(BF16)
    k = _rope_tiles(_dot(xb, win_ref[:, C_K:C_V]), cos, sin)
    v = _dot(xb, win_ref[:, C_V:C_Z])

    @pl.when(j == 0)
    def _():
        kmeta_ref[...] = k[META_PAD:, :]
        vmeta_ref[...] = v[META_PAD:, :]

    @pl.when(j == last)
    def _():
        kwin_ref[...] = k
        vwin_ref[...] = v

    def dup(t, g):
        tile = t[:, (g // 2) * LANES:(g // 2 + 1) * LANES]
        rolled = pltpu.roll(tile, HEAD_DIM, 1)
        return jnp.where(lo if g % 2 == 0 else jnp.logical_not(lo), tile, rolled)

    kd_cur = jnp.concatenate([dup(k, g) for g in range(N_KV_HEADS)], axis=1).astype(BF16)
    vd_cur = jnp.concatenate([dup(v, g) for g in range(N_KV_HEADS)], axis=1).astype(BF16)
    kd_sc[2 * T:3 * T, :] = kd_cur
    vd_sc[2 * T:3 * T, :] = vd_cur

    @pl.when(j == 0)
    def _():
        kd_sc[0:T, :] = kd_cur
        vd_sc[0:T, :] = vd_cur

    nq = Q_PER_KV * T
    r4 = lax.broadcasted_iota(jnp.int32, (nq, 3 * T), 0)
    c4 = lax.broadcasted_iota(jnp.int32, (nq, 3 * T), 1)
    rr = r4 % T
    ss = c4 % T
    kb = c4 // T
    mask = (((kb == 0) & (ss >= META_PAD) & ((j >= 1) | (ss <= rr)))
            | ((kb == 1) & (ss > rr) & (j >= 2))
            | ((kb == 2) & (ss <= rr) & (j >= 1)))
    rcol = lax.broadcasted_iota(jnp.int32, (nq, 1), 0)

    attn_tiles = []
    for g in range(N_KV_HEADS):
        qt0 = qb[:, (2 * g) * LANES:(2 * g + 1) * LANES]
        qt1 = qb[:, (2 * g + 1) * LANES:(2 * g + 2) * LANES]
        zero = jnp.zeros_like(qt0)
        lhs = jnp.concatenate([jnp.where(lo, qt0, zero), jnp.where(lo, zero, qt0),
                               jnp.where(lo, qt1, zero), jnp.where(lo, zero, qt1)], axis=0)
        s = _dot_nt(lhs, kd_sc[:, g * LANES:(g + 1) * LANES])
        s = jnp.where(mask, s, NEG_INF)
        sink = jnp.where(rcol < T, sinks_ref[4 * g],
                         jnp.where(rcol < 2 * T, sinks_ref[4 * g + 1],
                                   jnp.where(rcol < 3 * T, sinks_ref[4 * g + 2], sinks_ref[4 * g + 3])))
        m = jnp.maximum(jnp.max(s, axis=1, keepdims=True), sink)
        p = jnp.exp(s - m)
        den = jnp.sum(p, axis=1, keepdims=True) + jnp.exp(sink - m)
        o = _dot(p.astype(BF16), vd_sc[:, g * LANES:(g + 1) * LANES]) / den
        attn_tiles.append(jnp.where(lo, o[0:T], o[T:2 * T]))
        attn_tiles.append(jnp.where(lo, o[2 * T:3 * T], o[3 * T:4 * T]))
    y_attn = jnp.concatenate(attn_tiles, axis=1)

    kd_sc[T:2 * T, :] = kd_cur
    vd_sc[T:2 * T, :] = vd_cur

    z = _dot(xb, win_ref[:, C_Z:C_XBC])
    xbc_raw = _dot(xb, win_ref[:, C_XBC:C_GA])
    xbuf_sc[SUBLANES:SUBLANES + T, :] = xbc_raw
    acc = convb_ref[...]
    for kk in range(CONV_WIDTH):
        acc = acc + xbuf_sc[SUBLANES - kk:SUBLANES - kk + T, :] * convw_ref[CONV_WIDTH - 1 - kk:CONV_WIDTH - kk, :]

    @pl.when(j == last)
    def _():
        conv_ref[...] = xbuf_sc[SUBLANES + T - (CONV_WIDTH - 1):SUBLANES + T, :]

    xbuf_sc[0:SUBLANES, :] = xbuf_sc[T:T + SUBLANES, :]
    xc = _silu(acc)
    xs = xc[:, :SSM_INNER]
    bm = xc[:, SSM_INNER:SSM_INNER + GN].astype(BF16)
    cm = xc[:, SSM_INNER + GN:].astype(BF16)

    valid = (j * T + row - META_PAD) >= 0
    dt = jnp.where(valid, _softplus(_dot(xb, win_ref[:, C_DT:IN_PAD]) + dtb_ref[...]), 0.0)
    ad = dt * (-jnp.exp(alog_ref[...]))
    tril = row >= lane
    cs = _dot_exact_lhs(tril.astype(BF16), ad)
    cs_t = cs.T

    hp_row = lax.broadcasted_iota(jnp.int32, (LANES, LANES), 0) < SSM_HEAD_DIM
    y_tiles = []
    for i in range(SSM_HEADS // 2):
        g = i // 2
        h0, h1 = 2 * i, 2 * i + 1
        bg = bm[:, g * SSM_STATE:(g + 1) * SSM_STATE]
        cg = cm[:, g * SSM_STATE:(g + 1) * SSM_STATE]
        cb = _dot_nt(cg, bg)
        col0, col1 = cs[:, h0:h0 + 1], cs[:, h1:h1 + 1]
        row0, row1 = cs_t[h0:h0 + 1, :], cs_t[h1:h1 + 1, :]
        tot0, tot1 = cs_t[h0:h0 + 1, T - 1:T], cs_t[h1:h1 + 1, T - 1:T]
        m0 = (cb * jnp.exp(jnp.where(tril, col0 - row0, NEG_INF))).astype(BF16)
        m1 = (cb * jnp.exp(jnp.where(tril, col1 - row1, NEG_INF))).astype(BF16)
        xs_i = xs[:, i * LANES:(i + 1) * LANES]
        xd = xs_i * jnp.where(lo, dt[:, h0:h0 + 1], dt[:, h1:h1 + 1])
        xdb = xd.astype(BF16)
        y_diag = jnp.where(lo, _dot(m0, xdb), _dot(m1, xdb))
        s_pair = st_sc[i * LANES:(i + 1) * LANES, :]
        y_off = _dot_nt(cg, s_pair.astype(BF16)) * jnp.where(lo, jnp.exp(col0), jnp.exp(col1))
        wgt = jnp.where(lo, jnp.exp(tot0 - col0), jnp.exp(tot1 - col1))
        upd = _dot_tn((xd * wgt).astype(BF16), bg)
        st_sc[i * LANES:(i + 1) * LANES, :] = jnp.where(hp_row, jnp.exp(tot0), jnp.exp(tot1)) * s_pair + upd
        y_tiles.append(y_diag + y_off + dskip_ref[:, i * LANES:(i + 1) * LANES] * xs_i)
    y = jnp.concatenate(y_tiles, axis=1)

    @pl.when(j == last)
    def _():
        ssm_ref[...] = st_sc[...]

    y_ssd = _gated_rmsnorm(y, z, nw_ref[...])

    ga = _dot(xb, win_ref[:, C_GA:C_GS])
    gs = _dot(xb, win_ref[:, C_GS:C_DT])
    mixin = jax.nn.sigmoid(ga) * y_attn + jax.nn.sigmoid(gs) * y_ssd
    mix = _dot(mixin.astype(BF16), wo_ref[...])
    h1 = _layer_norm(ALPHA * x + mix, g1_ref[...], b1_ref[...])
    _to_tiles(h1_ref, h1)
    rl_ref[...] = _router_logits(h1, wrh_ref[...], wrl_ref[...], br_ref[...])


def _prompt_call(x_prompt, meta_blk, w_in_r, w_o_b, cos_t, sin_t, conv_w, conv_b, dtb, alog, dskip_e, nw,
                 g1, b1, wr_hi, wr_lo, br, sinks):
    B, S, D = x_prompt.shape
    nbs = S // BLOCK
    nb = nbs + 1
    tok = lambda b, j, *_: (b, jnp.maximum(j - 1, 0), 0)
    flat3 = lambda b, j, *_: (b * nbs + jnp.maximum(j - 1, 0), 0, 0)
    flat2 = lambda b, j, *_: (b * nbs + jnp.maximum(j - 1, 0), 0)
    perb = lambda b, j, *_: (b, 0, 0)
    grid_spec = pltpu.PrefetchScalarGridSpec(
        num_scalar_prefetch=1,
        grid=(B, nb),
        in_specs=[
            pl.BlockSpec((None, BLOCK, D), tok),
            _const_spec((BLOCK, D)),
            _const_spec((D, IN_PAD)),
            _const_spec((D, D)),
            pl.BlockSpec((BLOCK, LANES), lambda b, j, *_: (j, 0)),
            pl.BlockSpec((BLOCK, LANES), lambda b, j, *_: (j, 0)),
            _const_spec((CONV_WIDTH, CONV_DIM)),
            _const_spec((1, CONV_DIM)),
            _const_spec((1, LANES)),
            _const_spec((1, LANES)),
            _const_spec((1, SSM_INNER)),
            _const_spec((1, SSM_INNER)),
            _const_spec((1, D)),
            _const_spec((1, D)),
            _const_spec((D, LANES)),
            _const_spec((D, LANES)),
            _const_spec((1, LANES)),
        ],
        out_specs=[
            pl.BlockSpec((BLOCK, D_TILES, LANES), flat3),
            pl.BlockSpec((BLOCK, LANES), flat2),
            pl.BlockSpec((None, N_META, KV_DIM), perb),
            pl.BlockSpec((None, N_META, KV_DIM), perb),
            pl.BlockSpec((None, WINDOW, KV_DIM), perb),
            pl.BlockSpec((None, WINDOW, KV_DIM), perb),
            pl.BlockSpec((None, CONV_WIDTH - 1, CONV_DIM), perb),
            pl.BlockSpec((None, SSM_INNER, SSM_STATE), perb),
        ],
        scratch_shapes=[
            pltpu.VMEM((3 * BLOCK, N_KV_HEADS * LANES), BF16),
            pltpu.VMEM((3 * BLOCK, N_KV_HEADS * LANES), BF16),
            pltpu.VMEM((SUBLANES + BLOCK, CONV_DIM), F32),
            pltpu.VMEM((SSM_INNER, SSM_STATE), F32),
        ],
    )
    n_tok = B * S + TOKEN_BLOCK
    out_shape = [
        jax.ShapeDtypeStruct((n_tok, D_TILES, LANES), F32),
        jax.ShapeDtypeStruct((n_tok, LANES), F32),
        jax.ShapeDtypeStruct((B, N_META, KV_DIM), F32),
        jax.ShapeDtypeStruct((B, N_META, KV_DIM), F32),
        jax.ShapeDtypeStruct((B, WINDOW, KV_DIM), F32),
        jax.ShapeDtypeStruct((B, WINDOW, KV_DIM), F32),
        jax.ShapeDtypeStruct((B, CONV_WIDTH - 1, CONV_DIM), F32),
        jax.ShapeDtypeStruct((B, SSM_INNER, SSM_STATE), F32),
    ]
    return pl.pallas_call(
        _prompt_kernel, grid_spec=grid_spec, out_shape=out_shape, name="prompt_mixer",
        compiler_params=pltpu.CompilerParams(dimension_semantics=("arbitrary", "arbitrary"),
                                             vmem_limit_bytes=VMEM_LIMIT),
    )(sinks, x_prompt, meta_blk, w_in_r, w_o_b, cos_t, sin_t, conv_w, conv_b, dtb, alog, dskip_e, nw,
      g1, b1, wr_hi, wr_lo, br)


def _sample_proj_kernel(x_ref, win_ref, cos_ref, sin_ref, h0_ref, h1_ref, h2_ref, convw_ref, convb_ref,
                        dtb_ref, alog_ref, expand_ref,
                        q_ref, k_ref, v_ref, z_ref, xraw_ref, xdt_ref, dec_ref, xs_ref, b_ref, c_ref, ga_ref, gs_ref):
    xb = x_ref[...].astype(BF16)
    cos = cos_ref[...]
    sin = sin_ref[...]
    q_ref[...] = _rope_tiles(_dot(xb, win_ref[:, C_Q:C_K]), cos, sin) * (HEAD_DIM ** -0.5)
    k_ref[...] = _rope_tiles(_dot(xb, win_ref[:, C_K:C_V]), cos, sin)
    v_ref[...] = _dot(xb, win_ref[:, C_V:C_Z])
    z_ref[...] = _dot(xb, win_ref[:, C_Z:C_XBC])
    xraw = _dot(xb, win_ref[:, C_XBC:C_GA])
    xraw_ref[...] = xraw
    acc = (convb_ref[...] + h0_ref[...] * convw_ref[0:1, :] + h1_ref[...] * convw_ref[1:2, :]
           + h2_ref[...] * convw_ref[2:3, :] + xraw * convw_ref[3:4, :])
    xc = _silu(acc)
    xs = xc[:, :SSM_INNER]
    dt = _softplus(_dot(xb, win_ref[:, C_DT:IN_PAD]) + dtb_ref[...])
    ad = dt * (-jnp.exp(alog_ref[...]))
    expand = expand_ref[...]
    dt_e = _dot_exact_rhs(dt, expand)
    ad_e = _dot_exact_rhs(ad, expand)
    xs_ref[...] = xs
    xdt_ref[...] = xs * dt_e
    dec_ref[...] = jnp.exp(ad_e)
    b_ref[...] = xc[:, SSM_INNER:SSM_INNER + GN]
    c_ref[...] = xc[:, SSM_INNER + GN:]
    ga_ref[...] = _dot(xb, win_ref[:, C_GA:C_GS])
    gs_ref[...] = _dot(xb, win_ref[:, C_GS:C_DT])


def _sample_proj_call(xs2d, w_in_r, cos_s, sin_s, hist0, hist1, hist2, conv_w, conv_b, dtb, alog, expand):
    n = xs2d.shape[0]
    shapes = [(n, Q_DIM), (n, KV_DIM), (n, KV_DIM), (n, SSM_INNER), (n, CONV_DIM), (n, SSM_INNER), (n, SSM_INNER),
              (n, SSM_INNER), (n, GN), (n, GN), (n, D_MODEL), (n, D_MODEL)]
    return pl.pallas_call(
        _sample_proj_kernel,
        out_shape=[jax.ShapeDtypeStruct(s, F32) for s in shapes], name="sample_proj",
        compiler_params=pltpu.CompilerParams(vmem_limit_bytes=VMEM_LIMIT),
    )(xs2d, w_in_r, cos_s, sin_s, hist0, hist1, hist2, conv_w, conv_b, dtb, alog, expand)


def _sample_attn_kernel(qblk_ref, mk_ref, mv_ref, wk_ref, wv_ref, nk_ref, nv_ref, sink_ref, o_ref):
    qb = qblk_ref[...]
    qbb = qb.astype(BF16)
    bdot = lambda a, b, dims: lax.dot_general(a, b, dims, preferred_element_type=F32)
    nt = (((2,), (2,)), ((0,), (0,)))
    nn = (((2,), (1,)), ((0,), (0,)))
    wk = wk_ref[...].astype(BF16)
    wv = wv_ref[...].astype(BF16)
    mk = mk_ref[...].astype(BF16)
    mv = mv_ref[...].astype(BF16)
    nk = nk_ref[...]
    nv = nv_ref[...]
    s_win = bdot(qbb, wk, nt)
    s_meta = bdot(qbb, mk, nt)
    s_new = jnp.sum(qb * nk, axis=2, keepdims=True)
    w = wk_ref.shape[1]
    slot = lax.broadcasted_iota(jnp.int32, s_win.shape, 2)
    s_win = jnp.where((w - slot) < WINDOW, s_win, NEG_INF)
    sink = sink_ref[...]
    m = jnp.maximum(jnp.maximum(jnp.max(s_win, axis=2, keepdims=True), jnp.max(s_meta, axis=2, keepdims=True)),
                    jnp.maximum(s_new, sink))
    p_win = jnp.exp(s_win - m)
    p_meta = jnp.exp(s_meta - m)
    p_new = jnp.exp(s_new - m)
    den = (jnp.sum(p_win, axis=2, keepdims=True) + jnp.sum(p_meta, axis=2, keepdims=True) + p_new
           + jnp.exp(sink - m))
    o = (bdot(p_win.astype(BF16), wv, nn) + bdot(p_meta.astype(BF16), mv, nn)
         + p_new * nv)
    o_ref[...] = o / den


def _sample_attn_call(qblk, mk, mv, wk, wv, nk, nv, sink3, nblk):
    n = qblk.shape[0]
    w = wk.shape[1]
    b3 = lambda shape: pl.BlockSpec(shape, lambda i: (i, 0, 0))
    return pl.pallas_call(
        _sample_attn_kernel,
        grid=(n // nblk,),
        in_specs=[b3((nblk, N_HEADS, KV_DIM)), b3((nblk, N_META, KV_DIM)), b3((nblk, N_META, KV_DIM)),
                  b3((nblk, w, KV_DIM)), b3((nblk, w, KV_DIM)), b3((nblk, 1, KV_DIM)), b3((nblk, 1, KV_DIM)),
                  pl.BlockSpec((1, N_HEADS, 1), lambda i: (0, 0, 0))],
        out_specs=b3((nblk, N_HEADS, KV_DIM)),
        out_shape=jax.ShapeDtypeStruct((n, N_HEADS, KV_DIM), F32), name="sample_attn",
        compiler_params=pltpu.CompilerParams(dimension_semantics=("arbitrary",), vmem_limit_bytes=VMEM_LIMIT),
    )(qblk, mk, mv, wk, wv, nk, nv, sink3)


def _sample_ssd_kernel(xdt_ref, dec_ref, b_ref, c_ref, st_ref, yattn_ref, xs_ref, z_ref, ga_ref, gs_ref, x_ref,
                       dskip_ref, nw_ref, wo_ref, g1_ref, b1_ref, wrh_ref, wrl_ref, br_ref, h1_all_ref, rl_all_ref,
                       st_out_ref, h1_ref, rl_ref, xdt_t_sc, dec_t_sc, y_t_sc, *, nblk):
    del h1_all_ref, rl_all_ref
    i = pl.program_id(0)
    n_all = xdt_ref.shape[0]

    @pl.when(i == 0)
    def _():
        xdt_t_sc[...] = xdt_ref[...].T
        dec_t_sc[...] = dec_ref[...].T
        y_t_sc[...] = jnp.zeros_like(y_t_sc)

    r_id = lax.broadcasted_iota(jnp.int32, (n_all, LANES), 0)
    c_id = lax.broadcasted_iota(jnp.int32, (n_all, LANES), 1)
    for t in range(nblk):
        n = i * nblk + t
        pick_row = (r_id == n).astype(BF16)
        pick_col = (c_id == n).astype(BF16)
        xb_n = _dot_exact_rhs(xdt_t_sc[...], pick_row)
        db_n = _dot_exact_rhs(dec_t_sc[...], pick_row)
        brow = b_ref[pl.ds(n, 1), :]
        crow = c_ref[pl.ds(n, 1), :]
        gw = SSM_INNER // SSM_GROUPS
        bfull = jnp.concatenate([jnp.broadcast_to(brow[:, g * SSM_STATE:(g + 1) * SSM_STATE], (gw, SSM_STATE))
                                 for g in range(SSM_GROUPS)], axis=0)
        cfull = jnp.concatenate([jnp.broadcast_to(crow[:, g * SSM_STATE:(g + 1) * SSM_STATE], (gw, SSM_STATE))
                                 for g in range(SSM_GROUPS)], axis=0)
        h_new = st_ref[t] * db_n + xb_n * bfull
        st_out_ref[t] = h_new
        y_t_sc[...] += _dot_exact_rhs(h_new * cfull, pick_col)

    @pl.when(i == pl.num_programs(0) - 1)
    def _():
        xs = xs_ref[...]
        y = y_t_sc[...].T + dskip_ref[...] * xs
        y_ssd = _gated_rmsnorm(y, z_ref[...], nw_ref[...])
        mixin = jax.nn.sigmoid(ga_ref[...]) * yattn_ref[...] + jax.nn.sigmoid(gs_ref[...]) * y_ssd
        mix = _dot(mixin.astype(BF16), wo_ref[...])
        h1 = _layer_norm(ALPHA * x_ref[...] + mix, g1_ref[...], b1_ref[...])
        h1_ref[...] = jnp.zeros_like(h1_ref)
        rl_ref[...] = jnp.zeros_like(rl_ref)
        _to_tiles(h1_ref, h1, rows=n_all)
        rl_ref[0:n_all, :] = _router_logits(h1, wrh_ref[...], wrl_ref[...], br_ref[...])


def _sample_ssd_call(xdt, dec, bmat, cmat, state3, y_attn, xs, z, ga, gs, x2d, dskip_e, nw, w_o_b, g1, b1,
                     wr_hi, wr_lo, br, h1_all, rl_all, nblk):
    n = xdt.shape[0]
    full = lambda a: pl.BlockSpec(a.shape, lambda i: (0,) * a.ndim)
    st_spec = pl.BlockSpec((nblk, SSM_INNER, SSM_STATE), lambda i: (i, 0, 0))
    ins = [xdt, dec, bmat, cmat, state3, y_attn, xs, z, ga, gs, x2d, dskip_e, nw, w_o_b, g1, b1, wr_hi, wr_lo, br]
    in_specs = [full(a) for a in ins]
    in_specs[4] = st_spec
    in_specs += [pl.BlockSpec(memory_space=pl.ANY), pl.BlockSpec(memory_space=pl.ANY)]
    sample_blk = h1_all.shape[0] // TOKEN_BLOCK - 1
    return pl.pallas_call(
        functools.partial(_sample_ssd_kernel, nblk=nblk),
        grid=(n // nblk,),
        in_specs=in_specs,
        out_specs=[st_spec,
                   pl.BlockSpec((TOKEN_BLOCK, D_TILES, LANES), lambda i: (sample_blk, 0, 0)),
                   pl.BlockSpec((TOKEN_BLOCK, LANES), lambda i: (sample_blk, 0))],
        out_shape=[jax.ShapeDtypeStruct(state3.shape, F32), jax.ShapeDtypeStruct(h1_all.shape, F32),
                   jax.ShapeDtypeStruct(rl_all.shape, F32)],
        input_output_aliases={len(ins): 1, len(ins) + 1: 2},
        scratch_shapes=[pltpu.VMEM((SSM_INNER, n), F32), pltpu.VMEM((SSM_INNER, n), F32),
                        pltpu.VMEM((SSM_INNER, n), F32)],
        name="sample_ssd",
        compiler_params=pltpu.CompilerParams(dimension_semantics=("arbitrary",), vmem_limit_bytes=VMEM_LIMIT),
    )(*ins, h1_all, rl_all)


def _top2_experts(rl):
    lane = lax.broadcasted_iota(jnp.int32, rl.shape, 1)
    big = jnp.int32(1 << 20)
    is_g = lane < N_EXPERT_GROUPS
    gmax = jnp.max(jnp.where(is_g, rl, -jnp.inf), axis=1, keepdims=True)
    gsel = jnp.min(jnp.where(is_g & (rl == gmax), lane, big), axis=1, keepdims=True)
    gden = jnp.sum(jnp.where(is_g, jnp.exp(rl - gmax), 0.0), axis=1, keepdims=True)
    g_w = 1.0 / gden
    e_id = lane - ROUTE_OFF
    in_grp = (e_id >= 0) & (e_id < N_EXPERTS) & ((e_id // EXPERTS_PER_GROUP) == gsel)
    v1 = jnp.max(jnp.where(in_grp, rl, -jnp.inf), axis=1, keepdims=True)
    i1 = jnp.min(jnp.where(in_grp & (rl == v1), lane, big), axis=1, keepdims=True)
    rest = in_grp & (lane != i1)
    v2 = jnp.max(jnp.where(rest, rl, -jnp.inf), axis=1, keepdims=True)
    i2 = jnp.min(jnp.where(rest & (rl == v2), lane, big), axis=1, keepdims=True)
    e21 = jnp.exp(v2 - v1)
    p1 = g_w / (1.0 + e21)
    p2 = g_w * e21 / (1.0 + e21)
    return i1 - ROUTE_OFF, i2 - ROUTE_OFF, p1, p2


R_E1, R_E2, R_P1, R_P2, R_RANK1, R_RANK2 = range(6)


def _route_kernel(rl_ref, route_ref, counts_ref, tri_sc, run_sc):
    i = pl.program_id(0)
    tb = rl_ref.shape[0]

    @pl.when(i == 0)
    def _():
        r = lax.broadcasted_iota(jnp.int32, (tb, tb), 0)
        c = lax.broadcasted_iota(jnp.int32, (tb, tb), 1)
        tri_sc[...] = (r > c).astype(BF16)
        run_sc[...] = jnp.zeros_like(run_sc)

    e1, e2, p1, p2 = _top2_experts(rl_ref[...])
    lane = lax.broadcasted_iota(jnp.int32, (tb, LANES), 1)
    oh1 = (lane == e1).astype(F32)
    oh2 = (lane == e2).astype(F32)
    both = oh1 + oh2
    before = _dot(tri_sc[...], both.astype(BF16)) + run_sc[...]
    rank1 = jnp.sum(before * oh1, axis=1, keepdims=True)
    rank2 = jnp.sum(before * oh2, axis=1, keepdims=True)
    run_sc[...] += jnp.sum(both, axis=0, keepdims=True)
    rec = jnp.zeros((tb, LANES), F32)
    for k, val in ((R_E1, e1.astype(F32)), (R_E2, e2.astype(F32)), (R_P1, p1), (R_P2, p2),
                   (R_RANK1, rank1), (R_RANK2, rank2)):
        rec = jnp.where(lane == k, val, rec)
    route_ref[...] = rec
    counts_ref[...] = run_sc[...]


def _route_call(rl_all):
    t = rl_all.shape[0]
    tb = TOKEN_BLOCK
    return pl.pallas_call(
        _route_kernel,
        grid=(t // tb,),
        in_specs=[pl.BlockSpec((tb, LANES), lambda i: (i, 0))],
        out_specs=[pl.BlockSpec((tb, LANES), lambda i: (i, 0)), pl.BlockSpec((1, LANES), lambda i: (0, 0))],
        out_shape=[jax.ShapeDtypeStruct((t, LANES), F32), jax.ShapeDtypeStruct((1, LANES), F32)],
        scratch_shapes=[pltpu.VMEM((tb, tb), BF16), pltpu.VMEM((1, LANES), F32)],
        name="moe_route",
        compiler_params=pltpu.CompilerParams(dimension_semantics=("arbitrary",), vmem_limit_bytes=VMEM_LIMIT),
    )(rl_all)


def _routing_tables(route, counts_row, n_tiles):
    tr = EXPERT_ROWS
    counts = counts_row[0, :N_EXPERTS].astype(jnp.int32)
    padded = ((counts + tr - 1) // tr) * tr
    ends = jnp.cumsum(padded)
    starts = ends - padded
    e1 = route[:, R_E1].astype(jnp.int32)
    e2 = route[:, R_E2].astype(jnp.int32)
    pos1 = starts[e1] + route[:, R_RANK1].astype(jnp.int32)
    pos2 = starts[e2] + route[:, R_RANK2].astype(jnp.int32)
    n_used = ends[-1] // tr
    tile_id = jnp.arange(n_tiles, dtype=jnp.int32)
    tile_expert = jnp.sum((jnp.minimum(tile_id, n_used - 1)[:, None] * tr >= ends[None, :]).astype(jnp.int32), axis=1)
    m = jnp.arange(N_EXPERTS * tr, dtype=jnp.int32)
    pe, pk = m // tr, m % tr
    pad_rows = jnp.where(pk < (padded - counts)[pe], (starts + counts)[pe] + pk, n_tiles * tr + m)
    return pos1, pos2, tile_expert, n_used.reshape(1), pad_rows


def _row_copy(src, dst, sem):
    return pltpu.make_async_copy(src, dst, sem)


def _dispatch_kernel(pos1_ref, pos2_ref, pad_ref, h_hbm, xs_hbm, zero_sc, sem):
    i = pl.program_id(0)
    tb = TOKEN_BLOCK
    n_pad = pad_ref.shape[0]

    @pl.when(i == 0)
    def _():
        zero_sc[...] = jnp.zeros_like(zero_sc)

        def zstart(m, c):
            _row_copy(zero_sc, xs_hbm.at[pad_ref[m]], sem).start()
            return c

        def zwait(m, c):
            _row_copy(zero_sc, xs_hbm.at[pad_ref[m]], sem).wait()
            return c

        lax.fori_loop(0, n_pad, zstart, 0)
        lax.fori_loop(0, n_pad, zwait, 0)

    def start(r, c):
        t = i * tb + r
        _row_copy(h_hbm.at[t], xs_hbm.at[pos1_ref[t]], sem).start()
        _row_copy(h_hbm.at[t], xs_hbm.at[pos2_ref[t]], sem).start()
        return c

    def wait(r, c):
        t = i * tb + r
        _row_copy(h_hbm.at[t], xs_hbm.at[pos1_ref[t]], sem).wait()
        _row_copy(h_hbm.at[t], xs_hbm.at[pos2_ref[t]], sem).wait()
        return c

    lax.fori_loop(0, tb, start, 0)
    lax.fori_loop(0, tb, wait, 0)


def _dispatch_call(pos1, pos2, pad_rows, h1_all, n_rows):
    t = h1_all.shape[0]
    grid_spec = pltpu.PrefetchScalarGridSpec(
        num_scalar_prefetch=3,
        grid=(t // TOKEN_BLOCK,),
        in_specs=[pl.BlockSpec(memory_space=pl.ANY)],
        out_specs=pl.BlockSpec(memory_space=pl.ANY),
        scratch_shapes=[pltpu.VMEM((D_TILES, LANES), F32), pltpu.SemaphoreType.DMA(())],
    )
    return pl.pallas_call(
        _dispatch_kernel, grid_spec=grid_spec,
        out_shape=jax.ShapeDtypeStruct((n_rows, D_TILES, LANES), F32), name="moe_dispatch",
        compiler_params=pltpu.CompilerParams(dimension_semantics=("arbitrary",)),
    )(pos1, pos2, pad_rows, h1_all)


def _expert_kernel(te_ref, nu_ref, x_ref, wg_ref, wu_ref, wd_ref, y_ref):
    del te_ref

    @pl.when(pl.program_id(0) < nu_ref[0])
    def _():
        x = _from_tiles(x_ref).astype(BF16)
        hid = _silu(_dot(x, wg_ref[...])) * _dot(x, wu_ref[...])
        _to_tiles(y_ref, _dot(hid.astype(BF16), wd_ref[...]))


def _expert_call(tile_expert, n_used, xs, wg, wu, wd, n_tiles):
    tr = EXPERT_ROWS
    rows = lambda i, te, nu: (jnp.minimum(i, nu[0] - 1), 0, 0)
    wsel = lambda i, te, nu: (te[i], 0, 0)
    grid_spec = pltpu.PrefetchScalarGridSpec(
        num_scalar_prefetch=2,
        grid=(n_tiles,),
        in_specs=[pl.BlockSpec((tr, D_TILES, LANES), rows),
                  pl.BlockSpec((None, D_MODEL, D_EXPERT), wsel),
                  pl.BlockSpec((None, D_MODEL, D_EXPERT), wsel),
                  pl.BlockSpec((None, D_EXPERT, D_MODEL), wsel)],
        out_specs=pl.BlockSpec((tr, D_TILES, LANES), rows),
    )
    return pl.pallas_call(
        _expert_kernel, grid_spec=grid_spec,
        out_shape=jax.ShapeDtypeStruct((n_tiles * tr, D_TILES, LANES), F32), name="moe_experts",
        compiler_params=pltpu.CompilerParams(dimension_semantics=("arbitrary",), vmem_limit_bytes=VMEM_LIMIT),
    )(tile_expert, n_used, xs, wg, wu, wd)


def _combine_kernel(pos1_ref, pos2_ref, h_ref, route_ref, g2_ref, b2_ref, ys_hbm, op_ref, os_ref, ybuf, sem,
                    *, n_prompt_blocks):
    i = pl.program_id(0)
    n = pl.num_programs(0)
    tb = TOKEN_BLOCK

    def copies(blk, slot, r):
        t = blk * tb + r
        return (_row_copy(ys_hbm.at[pos1_ref[t]], ybuf.at[slot, 0, r], sem.at[slot]),
                _row_copy(ys_hbm.at[pos2_ref[t]], ybuf.at[slot, 1, r], sem.at[slot]))

    def issue(blk, slot):
        def body(r, c):
            a, b = copies(blk, slot, r)
            a.start()
            b.start()
            return c
        lax.fori_loop(0, tb, body, 0)

    def drain(blk, slot):
        def body(r, c):
            a, b = copies(blk, slot, r)
            a.wait()
            b.wait()
            return c
        lax.fori_loop(0, tb, body, 0)

    @pl.when(i == 0)
    def _():
        issue(0, 0)

    @pl.when(i + 1 < n)
    def _():
        issue(i + 1, (i + 1) % 2)

    slot = i % 2
    drain(i, slot)
    route = route_ref[...]
    p1 = route[:, R_P1:R_P1 + 1]
    p2 = route[:, R_P2:R_P2 + 1]
    h = _from_tiles(h_ref)
    moe = p1 * _from_tiles(ybuf.at[slot, 0]) + p2 * _from_tiles(ybuf.at[slot, 1])
    out = _layer_norm(ALPHA * h + moe, g2_ref[...], b2_ref[...])

    @pl.when(i < n_prompt_blocks)
    def _():
        op_ref[...] = out

    @pl.when(i >= n_prompt_blocks)
    def _():
        os_ref[...] = out


def _combine_call(pos1, pos2, h1_all, route, g2, b2, ys, n_prompt_tokens):
    t = h1_all.shape[0]
    tb = TOKEN_BLOCK
    npb = n_prompt_tokens // tb
    grid_spec = pltpu.PrefetchScalarGridSpec(
        num_scalar_prefetch=2,
        grid=(t // tb,),
        in_specs=[pl.BlockSpec((tb, D_TILES, LANES), lambda i, *_: (i, 0, 0)),
                  pl.BlockSpec((tb, LANES), lambda i, *_: (i, 0)),
                  pl.BlockSpec((1, D_MODEL), lambda i, *_: (0, 0)),
                  pl.BlockSpec((1, D_MODEL), lambda i, *_: (0, 0)),
                  pl.BlockSpec(memory_space=pl.ANY)],
        out_specs=[pl.BlockSpec((tb, D_MODEL), lambda i, *_: (jnp.minimum(i, npb - 1), 0)),
                   pl.BlockSpec((tb, D_MODEL), lambda i, *_: (0, 0))],
        scratch_shapes=[pltpu.VMEM((2, TOP_K, tb, D_TILES, LANES), F32), pltpu.SemaphoreType.DMA((2,))],
    )
    return pl.pallas_call(
        functools.partial(_combine_kernel, n_prompt_blocks=npb), grid_spec=grid_spec,
        out_shape=[jax.ShapeDtypeStruct((n_prompt_tokens, D_MODEL), F32),
                   jax.ShapeDtypeStruct((tb, D_MODEL), F32)],
        name="moe_combine",
        compiler_params=pltpu.CompilerParams(dimension_semantics=("arbitrary",), vmem_limit_bytes=VMEM_LIMIT),
    )(pos1, pos2, h1_all, route, g2, b2, ys)


def _rope_tables(pos):
    inv_freq = jnp.exp(-math.log(ROPE_THETA) * jnp.arange(HALF, dtype=F32) * (2.0 / HEAD_DIM))
    ang = pos.astype(F32)[:, None] * inv_freq[None, :]
    cos = jnp.cos(ang)
    sin = jnp.sin(ang)
    cos_t = jnp.tile(jnp.concatenate([cos, cos], axis=1), (1, LANES // HEAD_DIM))
    sin_t = jnp.tile(jnp.concatenate([-sin, sin], axis=1), (1, LANES // HEAD_DIM))
    return cos_t, sin_t


def _pad_lanes(a, n=LANES):
    return jnp.pad(a, ((0, 0), (0, n - a.shape[1])))


def kernel(x_prompt, x_sample, cache_meta_k, cache_meta_v, cache_win_k, cache_win_v, state_conv, state_ssm,
           meta_tokens, w_in, attn_sinks, conv_w, conv_b, dt_bias, a_log, d_skip, ssm_norm_w, w_o,
           ln1_g, ln1_b, router_group_w, router_group_b, router_expert_w, router_expert_b,
           w_gate, w_up, w_down, ln2_g, ln2_b):
    B, S, D = x_prompt.shape
    N = x_sample.shape[0]
    l = 0

    wi = w_in[l]
    dt0 = C_GA
    w_in_r = jnp.concatenate([wi[:, :dt0], wi[:, dt0 + SSM_HEADS:], wi[:, dt0:dt0 + SSM_HEADS],
                              jnp.zeros((D, LANES - SSM_HEADS), wi.dtype)], axis=1).astype(BF16)
    w_o_b = w_o[l].astype(BF16)
    wr = _pad_lanes(jnp.concatenate([router_group_w[l], router_expert_w[l]], axis=1))
    wr_hi = wr.astype(BF16)
    wr_lo = (wr - wr_hi.astype(F32)).astype(BF16)
    br = _pad_lanes(jnp.concatenate([router_group_b[l], router_expert_b[l]])[None, :])
    dtb = _pad_lanes(dt_bias[l][None, :])
    alog = _pad_lanes(a_log[l][None, :])
    dskip_e = jnp.repeat(d_skip[l], SSM_HEAD_DIM)[None, :]
    nw = ssm_norm_w[l][None, :]
    g1, b1 = ln1_g[l][None, :], ln1_b[l][None, :]
    g2, b2 = ln2_g[l][None, :], ln2_b[l][None, :]
    cw_, cb_ = conv_w[l], conv_b[l][None, :]
    sinks = attn_sinks[l]
    wg_b = w_gate[l].astype(BF16)
    wu_b = w_up[l].astype(BF16)
    wd_b = w_down[l].astype(BF16)

    nb = S // BLOCK + 1
    pos_p = jnp.arange(nb * BLOCK, dtype=jnp.int32) - META_PAD
    cos_p, sin_p = _rope_tables(pos_p)
    meta_blk = jnp.concatenate([jnp.zeros((META_PAD, D), F32), meta_tokens.astype(F32)], axis=0)
    (h1_all, rl_all, p_mk, p_mv, p_wk, p_wv, p_conv, p_ssm) = _prompt_call(
        x_prompt, meta_blk, w_in_r, w_o_b, cos_p, sin_p, cw_, cb_, dtb, alog, dskip_e, nw, g1, b1,
        wr_hi, wr_lo, br, sinks)

    xs2d = x_sample.reshape(N, D)
    cos_s, sin_s = _rope_tables(jnp.full((1,), PAST_LEN, jnp.int32))
    sc = state_conv[l]
    expand = (jnp.arange(LANES, dtype=jnp.int32)[:, None]
              == (jnp.arange(SSM_INNER, dtype=jnp.int32) // SSM_HEAD_DIM)[None, :]).astype(BF16)
    (q_s, k_s, v_s, z_s, xraw_s, xdt_s, dec_s, xsc_s, b_s, c_s, ga_s, gs_s) = _sample_proj_call(
        xs2d, w_in_r, cos_s, sin_s, sc[:, 0], sc[:, 1], sc[:, 2], cw_, cb_, dtb, alog, expand)
    head_group = jnp.arange(N_HEADS, dtype=jnp.int32) // Q_PER_KV
    grp_mask = (head_group[:, None] == jnp.arange(N_KV_HEADS, dtype=jnp.int32)[None, :]).astype(F32)
    qblk = (q_s.reshape(N, N_HEADS, 1, HEAD_DIM) * grp_mask[None, :, :, None]).reshape(N, N_HEADS, KV_DIM)
    wlen = cache_win_k.shape[2]
    o_blk = _sample_attn_call(
        qblk, cache_meta_k[l].reshape(N, N_META, KV_DIM), cache_meta_v[l].reshape(N, N_META, KV_DIM),
        cache_win_k[l].reshape(N, wlen, KV_DIM), cache_win_v[l].reshape(N, wlen, KV_DIM),
        k_s.reshape(N, 1, KV_DIM), v_s.reshape(N, 1, KV_DIM), sinks.reshape(1, N_HEADS, 1), 16)
    o4 = o_blk.reshape(N, N_HEADS, N_KV_HEADS, HEAD_DIM)
    y_attn_s = jnp.sum(o4 * grp_mask[None, :, :, None], axis=2).reshape(N, Q_DIM)
    s_ssm3, h1_all, rl_all = _sample_ssd_call(
        xdt_s, dec_s, b_s, c_s, state_ssm[l].reshape(N, SSM_INNER, SSM_STATE), y_attn_s, xsc_s, z_s, ga_s, gs_s,
        xs2d, dskip_e, nw, w_o_b, g1, b1, wr_hi, wr_lo, br, h1_all, rl_all, 8)

    n_tok = h1_all.shape[0]
    n_tiles = (TOP_K * n_tok) // EXPERT_ROWS + N_EXPERTS
    route, counts = _route_call(rl_all)
    pos1, pos2, tile_expert, n_used, pad_rows = _routing_tables(route, counts, n_tiles)
    xs_sorted = _dispatch_call(pos1, pos2, pad_rows, h1_all, (n_tiles + N_EXPERTS) * EXPERT_ROWS)
    ys_sorted = _expert_call(tile_expert, n_used, xs_sorted, wg_b, wu_b, wd_b, n_tiles)
    y_p2d, y_s2d = _combine_call(pos1, pos2, h1_all, route, g2, b2, ys_sorted, B * S)
    y_prompt = y_p2d.reshape(B, S, D)
    y_sample = y_s2d[:N].reshape(N, 1, D)

    kv5 = lambda a, n, t: a.reshape(1, n, t, N_KV_HEADS, HEAD_DIM)
    s_conv = jnp.stack([sc[:, 1], sc[:, 2], xraw_s], axis=1)[None]
    return (y_prompt, y_sample,
            kv5(p_mk, B, N_META), kv5(p_mv, B, N_META), kv5(p_wk, B, WINDOW), kv5(p_wv, B, WINDOW),
            p_conv[None], p_ssm.reshape(1, B, SSM_HEADS, SSM_HEAD_DIM, SSM_STATE),
            kv5(k_s, N, 1), kv5(v_s, N, 1), s_conv,
            s_ssm3.reshape(1, N, SSM_HEADS, SSM_HEAD_DIM, SSM_STATE))
```

```python
import functools
import math

import jax
import jax.numpy as jnp
from jax import lax
from jax.experimental import pallas as pl
from jax.experimental.pallas import tpu as pltpu

F32 = jnp.float32
BF16 = jnp.bfloat16

D_MODEL = 1024
N_META = 16
N_HEADS = 16
N_KV_HEADS = 4
HEAD_DIM = 64
HALF = HEAD_DIM // 2
Q_PER_KV = N_HEADS // N_KV_HEADS
WINDOW = 128
BLOCK = 128
META_PAD = BLOCK - N_META
ROPE_THETA = 10000.0
PAST_LEN = 16384
SSM_INNER = D_MODEL
SSM_HEAD_DIM = 64
SSM_HEADS = SSM_INNER // SSM_HEAD_DIM
SSM_GROUPS = 4
SSM_STATE = 128
CONV_WIDTH = 4
GN = SSM_GROUPS * SSM_STATE
CONV_DIM = SSM_INNER + 2 * GN
Q_DIM = N_HEADS * HEAD_DIM
KV_DIM = N_KV_HEADS * HEAD_DIM
N_EXPERT_GROUPS = 4
EXPERTS_PER_GROUP = 8
N_EXPERTS = N_EXPERT_GROUPS * EXPERTS_PER_GROUP
D_EXPERT = 256
DEPTH = 1
ALPHA = (2 * DEPTH) ** 0.25
LN_EPS = 1e-5
RMS_EPS = 1e-5
NEG_INF = -1e30

LANES = 128
SUBLANES = 8
VMEM_LIMIT = 56 * 1024 * 1024

C_Q = 0
C_K = C_Q + Q_DIM
C_V = C_K + KV_DIM
C_Z = C_V + KV_DIM
C_XBC = C_Z + SSM_INNER
C_GA = C_XBC + CONV_DIM
C_GS = C_GA + D_MODEL
C_DT = C_GS + D_MODEL
IN_PAD = C_DT + LANES
ROUTE_OFF = N_EXPERT_GROUPS
D_TILES = D_MODEL // LANES
TOKEN_BLOCK = 512
EXPERT_ROWS = 256
TOP_K = 2


def _to_tiles(ref, x):
    n = x.shape[0]
    for c in range(D_TILES):
        ref[pl.ds(c, n, stride=D_TILES), :] = x[:, c * LANES:(c + 1) * LANES]


def _from_tiles(ref, n):
    return jnp.concatenate([ref[pl.ds(c, n, stride=D_TILES), :] for c in range(D_TILES)], axis=1)


def _token_rows(t):
    return pl.ds(pl.multiple_of(t * D_TILES, D_TILES), D_TILES)


def _const_spec(shape):
    nd = len(shape)
    return pl.BlockSpec(shape, lambda *_: (0,) * nd, pipeline_mode=pl.Buffered(1))


def _dot(a, b):
    return jnp.dot(a, b, preferred_element_type=F32)


def _dot_nt(a, b):
    return lax.dot_general(a, b, (((1,), (1,)), ((), ())), preferred_element_type=F32)


def _dot_tn(a, b):
    return lax.dot_general(a, b, (((0,), (0,)), ((), ())), preferred_element_type=F32)


def _split3(x):
    p1 = x.astype(BF16)
    r1 = x - p1.astype(F32)
    p2 = r1.astype(BF16)
    r2 = r1 - p2.astype(F32)
    return p1, p2, r2.astype(BF16)


def _dot_exact_rhs(x, m_bf16):
    p1, p2, p3 = _split3(x)
    return _dot(p1, m_bf16) + _dot(p2, m_bf16) + _dot(p3, m_bf16)


def _dot_exact_lhs(m_bf16, x):
    p1, p2, p3 = _split3(x)
    return _dot(m_bf16, p1) + _dot(m_bf16, p2) + _dot(m_bf16, p3)


def _silu(x):
    return x * jax.nn.sigmoid(x)


def _softplus(x):
    return jnp.maximum(x, 0.0) + jnp.log1p(jnp.exp(-jnp.abs(x)))


def _layer_norm(x, g, b):
    mu = jnp.mean(x, -1, keepdims=True)
    xc = x - mu
    var = jnp.mean(xc * xc, -1, keepdims=True)
    return xc * lax.rsqrt(var + LN_EPS) * g + b


def _rope_tiles(t, cos, sin_signed):
    lane = lax.broadcasted_iota(jnp.int32, (t.shape[0], LANES), 1)
    first_half = (lane % HEAD_DIM) < HALF
    outs = []
    for i in range(t.shape[1] // LANES):
        x = t[:, i * LANES:(i + 1) * LANES]
        partner = jnp.where(first_half, pltpu.roll(x, LANES - HALF, 1), pltpu.roll(x, HALF, 1))
        outs.append(x * cos + partner * sin_signed)
    return jnp.concatenate(outs, axis=1) if len(outs) > 1 else outs[0]


def _gated_rmsnorm(y, z, nw):
    yf = y * _silu(z)
    gw = SSM_INNER // SSM_GROUPS
    outs = []
    for g in range(SSM_GROUPS):
        t = yf[:, g * gw:(g + 1) * gw]
        ms = jnp.mean(t * t, -1, keepdims=True)
        outs.append(t * lax.rsqrt(ms + RMS_EPS))
    return jnp.concatenate(outs, axis=1) * nw


def _router_logits(h, wr_hi, wr_lo, br):
    h_hi = h.astype(BF16)
    h_lo = (h - h_hi.astype(F32)).astype(BF16)
    return _dot(h_hi, wr_hi) + _dot(h_lo, wr_hi) + _dot(h_hi, wr_lo) + br


def _prompt_kernel(sinks_ref, x_ref, meta_ref, win_ref, wo_ref, cos_ref, sin_ref, convw_ref, convb_ref,
                   dtb_ref, alog_ref, dskip_ref, nw_ref, g1_ref, b1_ref, wrh_ref, wrl_ref, br_ref,
                   h1_ref, rl_ref, kmeta_ref, vmeta_ref, kwin_ref, vwin_ref, conv_ref, ssm_ref,
                   kd_sc, vd_sc, xbuf_sc, st_sc):
    j = pl.program_id(1)
    last = pl.num_programs(1) - 1
    T = BLOCK

    @pl.when(j == 0)
    def _():
        kd_sc[...] = jnp.zeros_like(kd_sc)
        vd_sc[...] = jnp.zeros_like(vd_sc)
        xbuf_sc[0:SUBLANES, :] = jnp.zeros((SUBLANES, CONV_DIM), F32)
        st_sc[...] = jnp.zeros_like(st_sc)

    x = jnp.where(j == 0, meta_ref[...], x_ref[...])
    xb = x.astype(BF16)

    lane = lax.broadcasted_iota(jnp.int32, (T, LANES), 1)
    row = lax.broadcasted_iota(jnp.int32, (T, LANES), 0)
    lo = lane < HEAD_DIM

    cos = cos_ref[...]
    sin = sin_ref[...]

    q = _rope_tiles(_dot(xb, win_ref[:, C_Q:C_K]), cos, sin)
    qb = (q * (HEAD_DIM ** -0.5)).astype(BF16)
    k = _rope_tiles(_dot(xb, win_ref[:, C_K:C_V]), cos, sin)
    v = _dot(xb, win_ref[:, C_V:C_Z])

    @pl.when(j == 0)
    def _():
        kmeta_ref[...] = k[META_PAD:, :]
        vmeta_ref[...] = v[META_PAD:, :]

    @pl.when(j == last)
    def _():
        kwin_ref[...] = k
        vwin_ref[...] = v

    def dup(t, g):
        tile = t[:, (g // 2) * LANES:(g // 2 + 1) * LANES]
        rolled = pltpu.roll(tile, HEAD_DIM, 1)
        return jnp.where(lo if g % 2 == 0 else jnp.logical_not(lo), tile, rolled)

    kd_cur = jnp.concatenate([dup(k, g) for g in range(N_KV_HEADS)], axis=1).astype(BF16)
    vd_cur = jnp.concatenate([dup(v, g) for g in range(N_KV_HEADS)], axis=1).astype(BF16)
    kd_sc[2 * T:3 * T, :] = kd_cur
    vd_sc[2 * T:3 * T, :] = vd_cur

    @pl.when(j == 0)
    def _():
        kd_sc[0:T, :] = kd_cur
        vd_sc[0:T, :] = vd_cur

    nq = Q_PER_KV * T
    r4 = lax.broadcasted_iota(jnp.int32, (nq, 3 * T), 0)
    c4 = lax.broadcasted_iota(jnp.int32, (nq, 3 * T), 1)
    rr = r4 % T
    ss = c4 % T
    kb = c4 // T
    mask = (((kb == 0) & (ss >= META_PAD) & ((j >= 1) | (ss <= rr)))
            | ((kb == 1) & (ss > rr) & (j >= 2))
            | ((kb == 2) & (ss <= rr) & (j >= 1)))
    rcol = lax.broadcasted_iota(jnp.int32, (nq, 1), 0)

    attn_tiles = []
    for g in range(N_KV_HEADS):
        qt0 = qb[:, (2 * g) * LANES:(2 * g + 1) * LANES]
        qt1 = qb[:, (2 * g + 1) * LANES:(2 * g + 2) * LANES]
        zero = jnp.zeros_like(qt0)
        lhs = jnp.concatenate([jnp.where(lo, qt0, zero), jnp.where(lo, zero, qt0),
                               jnp.where(lo, qt1, zero), jnp.where(lo, zero, qt1)], axis=0)
        s = _dot_nt(lhs, kd_sc[:, g * LANES:(g + 1) * LANES])
        s = jnp.where(mask, s, NEG_INF)
        sink = jnp.where(rcol < T, sinks_ref[4 * g],
                         jnp.where(rcol < 2 * T, sinks_ref[4 * g + 1],
                                   jnp.where(rcol < 3 * T, sinks_ref[4 * g + 2], sinks_ref[4 * g + 3])))
        m = jnp.maximum(jnp.max(s, axis=1, keepdims=True), sink)
        p = jnp.exp(s - m)
        den = jnp.sum(p, axis=1, keepdims=True) + jnp.exp(sink - m)
        o = _dot(p.astype(BF16), vd_sc[:, g * LANES:(g + 1) * LANES]) / den
        attn_tiles.append(jnp.where(lo, o[0:T], o[T:2 * T]))
        attn_tiles.append(jnp.where(lo, o[2 * T:3 * T], o[3 * T:4 * T]))
    y_attn = jnp.concatenate(attn_tiles, axis=1)

    kd_sc[T:2 * T, :] = kd_cur
    vd_sc[T:2 * T, :] = vd_cur

    z = _dot(xb, win_ref[:, C_Z:C_XBC])
    xbc_raw = _dot(xb, win_ref[:, C_XBC:C_GA])
    xbuf_sc[SUBLANES:SUBLANES + T, :] = xbc_raw
    acc = convb_ref[...]
    for kk in range(CONV_WIDTH):
        acc = acc + xbuf_sc[SUBLANES - kk:SUBLANES - kk + T, :] * convw_ref[CONV_WIDTH - 1 - kk:CONV_WIDTH - kk, :]

    @pl.when(j == last)
    def _():
        conv_ref[...] = xbuf_sc[SUBLANES + T - (CONV_WIDTH - 1):SUBLANES + T, :]

    xbuf_sc[0:SUBLANES, :] = xbuf_sc[T:T + SUBLANES, :]
    xc = _silu(acc)
    xs = xc[:, :SSM_INNER]
    bm = xc[:, SSM_INNER:SSM_INNER + GN].astype(BF16)
    cm = xc[:, SSM_INNER + GN:].astype(BF16)

    valid = (j * T + row - META_PAD) >= 0
    dt = jnp.where(valid, _softplus(_dot(xb, win_ref[:, C_DT:IN_PAD]) + dtb_ref[...]), 0.0)
    ad = dt * (-jnp.exp(alog_ref[...]))
    tril = row >= lane
    cs = _dot_exact_lhs(tril.astype(BF16), ad)
    cs_t = cs.T

    hp_row = lax.broadcasted_iota(jnp.int32, (LANES, LANES), 0) < SSM_HEAD_DIM
    y_tiles = []
    for i in range(SSM_HEADS // 2):
        g = i // 2
        h0, h1 = 2 * i, 2 * i + 1
        bg = bm[:, g * SSM_STATE:(g + 1) * SSM_STATE]
        cg = cm[:, g * SSM_STATE:(g + 1) * SSM_STATE]
        cb = _dot_nt(cg, bg)
        col0, col1 = cs[:, h0:h0 + 1], cs[:, h1:h1 + 1]
        row0, row1 = cs_t[h0:h0 + 1, :], cs_t[h1:h1 + 1, :]
        tot0, tot1 = cs_t[h0:h0 + 1, T - 1:T], cs_t[h1:h1 + 1, T - 1:T]
        m0 = (cb * jnp.exp(jnp.where(tril, col0 - row0, NEG_INF))).astype(BF16)
        m1 = (cb * jnp.exp(jnp.where(tril, col1 - row1, NEG_INF))).astype(BF16)
        xs_i = xs[:, i * LANES:(i + 1) * LANES]
        xd = xs_i * jnp.where(lo, dt[:, h0:h0 + 1], dt[:, h1:h1 + 1])
        xdb = xd.astype(BF16)
        y_diag = jnp.where(lo, _dot(m0, xdb), _dot(m1, xdb))
        s_pair = st_sc[i * LANES:(i + 1) * LANES, :]
        y_off = _dot_nt(cg, s_pair.astype(BF16)) * jnp.where(lo, jnp.exp(col0), jnp.exp(col1))
        wgt = jnp.where(lo, jnp.exp(tot0 - col0), jnp.exp(tot1 - col1))
        upd = _dot_tn((xd * wgt).astype(BF16), bg)
        st_sc[i * LANES:(i + 1) * LANES, :] = jnp.where(hp_row, jnp.exp(tot0), jnp.exp(tot1)) * s_pair + upd
        y_tiles.append(y_diag + y_off + dskip_ref[:, i * LANES:(i + 1) * LANES] * xs_i)
    y = jnp.concatenate(y_tiles, axis=1)

    @pl.when(j == last)
    def _():
        ssm_ref[...] = st_sc[...]

    y_ssd = _gated_rmsnorm(y, z, nw_ref[...])

    ga = _dot(xb, win_ref[:, C_GA:C_GS])
    gs = _dot(xb, win_ref[:, C_GS:C_DT])
    mixin = jax.nn.sigmoid(ga) * y_attn + jax.nn.sigmoid(gs) * y_ssd
    mix = _dot(mixin.astype(BF16), wo_ref[...])
    h1 = _layer_norm(ALPHA * x + mix, g1_ref[...], b1_ref[...])
    _to_tiles(h1_ref, h1)
    rl_ref[...] = _router_logits(h1, wrh_ref[...], wrl_ref[...], br_ref[...])


def _prompt_call(x_prompt, meta_blk, w_in_r, w_o_b, cos_t, sin_t, conv_w, conv_b, dtb, alog, dskip_e, nw,
                 g1, b1, wr_hi, wr_lo, br, sinks):
    B, S, D = x_prompt.shape
    nbs = S // BLOCK
    nb = nbs + 1
    tok = lambda b, j, *_: (b, jnp.maximum(j - 1, 0), 0)
    flat2 = lambda b, j, *_: (b * nbs + jnp.maximum(j - 1, 0), 0)
    perb = lambda b, j, *_: (b, 0, 0)
    grid_spec = pltpu.PrefetchScalarGridSpec(
        num_scalar_prefetch=1,
        grid=(B, nb),
        in_specs=[
            pl.BlockSpec((None, BLOCK, D), tok),
            _const_spec((BLOCK, D)),
            _const_spec((D, IN_PAD)),
            _const_spec((D, D)),
            pl.BlockSpec((BLOCK, LANES), lambda b, j, *_: (j, 0)),
            pl.BlockSpec((BLOCK, LANES), lambda b, j, *_: (j, 0)),
            _const_spec((CONV_WIDTH, CONV_DIM)),
            _const_spec((1, CONV_DIM)),
            _const_spec((1, LANES)),
            _const_spec((1, LANES)),
            _const_spec((1, SSM_INNER)),
            _const_spec((1, SSM_INNER)),
            _const_spec((1, D)),
            _const_spec((1, D)),
            _const_spec((D, LANES)),
            _const_spec((D, LANES)),
            _const_spec((1, LANES)),
        ],
        out_specs=[
            pl.BlockSpec((BLOCK * D_TILES, LANES), flat2),
            pl.BlockSpec((BLOCK, LANES), flat2),
            pl.BlockSpec((None, N_META, KV_DIM), perb),
            pl.BlockSpec((None, N_META, KV_DIM), perb),
            pl.BlockSpec((None, WINDOW, KV_DIM), perb),
            pl.BlockSpec((None, WINDOW, KV_DIM), perb),
            pl.BlockSpec((None, CONV_WIDTH - 1, CONV_DIM), perb),
            pl.BlockSpec((None, SSM_INNER, SSM_STATE), perb),
        ],
        scratch_shapes=[
            pltpu.VMEM((3 * BLOCK, N_KV_HEADS * LANES), BF16),
            pltpu.VMEM((3 * BLOCK, N_KV_HEADS * LANES), BF16),
            pltpu.VMEM((SUBLANES + BLOCK, CONV_DIM), F32),
            pltpu.VMEM((SSM_INNER, SSM_STATE), F32),
        ],
    )
    n_tok = B * S
    out_shape = [
        jax.ShapeDtypeStruct((n_tok * D_TILES, LANES), F32),
        jax.ShapeDtypeStruct((n_tok, LANES), F32),
        jax.ShapeDtypeStruct((B, N_META, KV_DIM), F32),
        jax.ShapeDtypeStruct((B, N_META, KV_DIM), F32),
        jax.ShapeDtypeStruct((B, WINDOW, KV_DIM), F32),
        jax.ShapeDtypeStruct((B, WINDOW, KV_DIM), F32),
        jax.ShapeDtypeStruct((B, CONV_WIDTH - 1, CONV_DIM), F32),
        jax.ShapeDtypeStruct((B, SSM_INNER, SSM_STATE), F32),
    ]
    return pl.pallas_call(
        _prompt_kernel, grid_spec=grid_spec, out_shape=out_shape, name="prompt_mixer",
        compiler_params=pltpu.CompilerParams(dimension_semantics=("arbitrary", "arbitrary"),
                                             vmem_limit_bytes=VMEM_LIMIT),
    )(sinks, x_prompt, meta_blk, w_in_r, w_o_b, cos_t, sin_t, conv_w, conv_b, dtb, alog, dskip_e, nw,
      g1, b1, wr_hi, wr_lo, br)


def _sample_proj_kernel(x_ref, win_ref, cos_ref, sin_ref, h0_ref, h1_ref, h2_ref, convw_ref, convb_ref,
                        dtb_ref, alog_ref, expand_ref,
                        q_ref, k_ref, v_ref, z_ref, xraw_ref, xdt_ref, dec_ref, xs_ref, b_ref, c_ref, ga_ref, gs_ref):
    xb = x_ref[...].astype(BF16)
    cos = cos_ref[...]
    sin = sin_ref[...]
    q_ref[...] = _rope_tiles(_dot(xb, win_ref[:, C_Q:C_K]), cos, sin) * (HEAD_DIM ** -0.5)
    k_ref[...] = _rope_tiles(_dot(xb, win_ref[:, C_K:C_V]), cos, sin)
    v_ref[...] = _dot(xb, win_ref[:, C_V:C_Z])
    z_ref[...] = _dot(xb, win_ref[:, C_Z:C_XBC])
    xraw = _dot(xb, win_ref[:, C_XBC:C_GA])
    xraw_ref[...] = xraw
    acc = (convb_ref[...] + h0_ref[...] * convw_ref[0:1, :] + h1_ref[...] * convw_ref[1:2, :]
           + h2_ref[...] * convw_ref[2:3, :] + xraw * convw_ref[3:4, :])
    xc = _silu(acc)
    xs = xc[:, :SSM_INNER]
    dt = _softplus(_dot(xb, win_ref[:, C_DT:IN_PAD]) + dtb_ref[...])
    ad = dt * (-jnp.exp(alog_ref[...]))
    expand = expand_ref[...]
    dt_e = _dot_exact_rhs(dt, expand)
    ad_e = _dot_exact_rhs(ad, expand)
    xs_ref[...] = xs
    xdt_ref[...] = xs * dt_e
    dec_ref[...] = jnp.exp(ad_e)
    b_ref[...] = xc[:, SSM_INNER:SSM_INNER + GN]
    c_ref[...] = xc[:, SSM_INNER + GN:]
    ga_ref[...] = _dot(xb, win_ref[:, C_GA:C_GS])
    gs_ref[...] = _dot(xb, win_ref[:, C_GS:C_DT])


def _sample_proj_call(xs2d, w_in_r, cos_s, sin_s, hist0, hist1, hist2, conv_w, conv_b, dtb, alog, expand):
    n = xs2d.shape[0]
    shapes = [(n, Q_DIM), (n, KV_DIM), (n, KV_DIM), (n, SSM_INNER), (n, CONV_DIM), (n, SSM_INNER), (n, SSM_INNER),
              (n, SSM_INNER), (n, GN), (n, GN), (n, D_MODEL), (n, D_MODEL)]
    return pl.pallas_call(
        _sample_proj_kernel,
        out_shape=[jax.ShapeDtypeStruct(s, F32) for s in shapes], name="sample_proj",
        compiler_params=pltpu.CompilerParams(vmem_limit_bytes=VMEM_LIMIT),
    )(xs2d, w_in_r, cos_s, sin_s, hist0, hist1, hist2, conv_w, conv_b, dtb, alog, expand)


def _sample_attn_kernel(qblk_ref, mk_ref, mv_ref, wk_ref, wv_ref, nk_ref, nv_ref, sink_ref, o_ref):
    qb = qblk_ref[...]
    qbb = qb.astype(BF16)
    bdot = lambda a, b, dims: lax.dot_general(a, b, dims, preferred_element_type=F32)
    nt = (((2,), (2,)), ((0,), (0,)))
    nn = (((2,), (1,)), ((0,), (0,)))
    wk = wk_ref[...].astype(BF16)
    wv = wv_ref[...].astype(BF16)
    mk = mk_ref[...].astype(BF16)
    mv = mv_ref[...].astype(BF16)
    nk = nk_ref[...]
    nv = nv_ref[...]
    s_win = bdot(qbb, wk, nt)
    s_meta = bdot(qbb, mk, nt)
    s_new = jnp.sum(qb * nk, axis=2, keepdims=True)
    w = wk_ref.shape[1]
    slot = lax.broadcasted_iota(jnp.int32, s_win.shape, 2)
    s_win = jnp.where((w - slot) < WINDOW, s_win, NEG_INF)
    sink = sink_ref[...]
    m = jnp.maximum(jnp.maximum(jnp.max(s_win, axis=2, keepdims=True), jnp.max(s_meta, axis=2, keepdims=True)),
                    jnp.maximum(s_new, sink))
    p_win = jnp.exp(s_win - m)
    p_meta = jnp.exp(s_meta - m)
    p_new = jnp.exp(s_new - m)
    den = (jnp.sum(p_win, axis=2, keepdims=True) + jnp.sum(p_meta, axis=2, keepdims=True) + p_new
           + jnp.exp(sink - m))
    o = (bdot(p_win.astype(BF16), wv, nn) + bdot(p_meta.astype(BF16), mv, nn)
         + p_new * nv)
    o_ref[...] = o / den


def _sample_attn_call(qblk, mk, mv, wk, wv, nk, nv, sink3, nblk):
    n = qblk.shape[0]
    w = wk.shape[1]
    b3 = lambda shape: pl.BlockSpec(shape, lambda i: (i, 0, 0))
    return pl.pallas_call(
        _sample_attn_kernel,
        grid=(n // nblk,),
        in_specs=[b3((nblk, N_HEADS, KV_DIM)), b3((nblk, N_META, KV_DIM)), b3((nblk, N_META, KV_DIM)),
                  b3((nblk, w, KV_DIM)), b3((nblk, w, KV_DIM)), b3((nblk, 1, KV_DIM)), b3((nblk, 1, KV_DIM)),
                  pl.BlockSpec((1, N_HEADS, 1), lambda i: (0, 0, 0))],
        out_specs=b3((nblk, N_HEADS, KV_DIM)),
        out_shape=jax.ShapeDtypeStruct((n, N_HEADS, KV_DIM), F32), name="sample_attn",
        compiler_params=pltpu.CompilerParams(dimension_semantics=("arbitrary",), vmem_limit_bytes=VMEM_LIMIT),
    )(qblk, mk, mv, wk, wv, nk, nv, sink3)


def _sample_ssd_kernel(xdt_ref, dec_ref, b_ref, c_ref, st_ref, yattn_ref, xs_ref, z_ref, ga_ref, gs_ref, x_ref,
                       dskip_ref, nw_ref, wo_ref, g1_ref, b1_ref, wrh_ref, wrl_ref, br_ref,
                       st_out_ref, h1_ref, rl_ref, xdt_t_sc, dec_t_sc, y_t_sc, *, nblk):
    i = pl.program_id(0)
    n_all = xdt_ref.shape[0]

    @pl.when(i == 0)
    def _():
        xdt_t_sc[...] = xdt_ref[...].T
        dec_t_sc[...] = dec_ref[...].T
        y_t_sc[...] = jnp.zeros_like(y_t_sc)

    r_id = lax.broadcasted_iota(jnp.int32, (n_all, LANES), 0)
    c_id = lax.broadcasted_iota(jnp.int32, (n_all, LANES), 1)
    for t in range(nblk):
        n = i * nblk + t
        pick_row = (r_id == n).astype(BF16)
        pick_col = (c_id == n).astype(BF16)
        xb_n = _dot_exact_rhs(xdt_t_sc[...], pick_row)
        db_n = _dot_exact_rhs(dec_t_sc[...], pick_row)
        brow = b_ref[pl.ds(n, 1), :]
        crow = c_ref[pl.ds(n, 1), :]
        gw = SSM_INNER // SSM_GROUPS
        bfull = jnp.concatenate([jnp.broadcast_to(brow[:, g * SSM_STATE:(g + 1) * SSM_STATE], (gw, SSM_STATE))
                                 for g in range(SSM_GROUPS)], axis=0)
        cfull = jnp.concatenate([jnp.broadcast_to(crow[:, g * SSM_STATE:(g + 1) * SSM_STATE], (gw, SSM_STATE))
                                 for g in range(SSM_GROUPS)], axis=0)
        h_new = st_ref[t] * db_n + xb_n * bfull
        st_out_ref[t] = h_new
        y_t_sc[...] += _dot_exact_rhs(h_new * cfull, pick_col)

    @pl.when(i == pl.num_programs(0) - 1)
    def _():
        xs = xs_ref[...]
        y = y_t_sc[...].T + dskip_ref[...] * xs
        y_ssd = _gated_rmsnorm(y, z_ref[...], nw_ref[...])
        mixin = jax.nn.sigmoid(ga_ref[...]) * yattn_ref[...] + jax.nn.sigmoid(gs_ref[...]) * y_ssd
        mix = _dot(mixin.astype(BF16), wo_ref[...])
        h1 = _layer_norm(ALPHA * x_ref[...] + mix, g1_ref[...], b1_ref[...])
        h1_ref[...] = jnp.zeros_like(h1_ref)
        rl_ref[...] = jnp.zeros_like(rl_ref)
        _to_tiles(h1_ref, h1)
        rl_ref[0:n_all, :] = _router_logits(h1, wrh_ref[...], wrl_ref[...], br_ref[...])


def _sample_ssd_call(xdt, dec, bmat, cmat, state3, y_attn, xs, z, ga, gs, x2d, dskip_e, nw, w_o_b, g1, b1,
                     wr_hi, wr_lo, br, nblk):
    n = xdt.shape[0]
    full = lambda a: pl.BlockSpec(a.shape, lambda i: (0,) * a.ndim)
    st_spec = pl.BlockSpec((nblk, SSM_INNER, SSM_STATE), lambda i: (i, 0, 0))
    ins = [xdt, dec, bmat, cmat, state3, y_attn, xs, z, ga, gs, x2d, dskip_e, nw, w_o_b, g1, b1, wr_hi, wr_lo, br]
    in_specs = [full(a) for a in ins]
    in_specs[4] = st_spec
    return pl.pallas_call(
        functools.partial(_sample_ssd_kernel, nblk=nblk),
        grid=(n // nblk,),
        in_specs=in_specs,
        out_specs=[st_spec,
                   pl.BlockSpec((TOKEN_BLOCK * D_TILES, LANES), lambda i: (0, 0)),
                   pl.BlockSpec((TOKEN_BLOCK, LANES), lambda i: (0, 0))],
        out_shape=[jax.ShapeDtypeStruct(state3.shape, F32),
                   jax.ShapeDtypeStruct((TOKEN_BLOCK * D_TILES, LANES), F32),
                   jax.ShapeDtypeStruct((TOKEN_BLOCK, LANES), F32)],
        scratch_shapes=[pltpu.VMEM((SSM_INNER, n), F32), pltpu.VMEM((SSM_INNER, n), F32),
                        pltpu.VMEM((SSM_INNER, n), F32)],
        name="sample_ssd",
        compiler_params=pltpu.CompilerParams(dimension_semantics=("arbitrary",), vmem_limit_bytes=VMEM_LIMIT),
    )(*ins)


def _top2_experts(rl):
    lane = lax.broadcasted_iota(jnp.int32, rl.shape, 1)
    big = jnp.int32(1 << 20)
    is_g = lane < N_EXPERT_GROUPS
    gmax = jnp.max(jnp.where(is_g, rl, -jnp.inf), axis=1, keepdims=True)
    gsel = jnp.min(jnp.where(is_g & (rl == gmax), lane, big), axis=1, keepdims=True)
    gden = jnp.sum(jnp.where(is_g, jnp.exp(rl - gmax), 0.0), axis=1, keepdims=True)
    g_w = 1.0 / gden
    e_id = lane - ROUTE_OFF
    in_grp = (e_id >= 0) & (e_id < N_EXPERTS) & ((e_id // EXPERTS_PER_GROUP) == gsel)
    v1 = jnp.max(jnp.where(in_grp, rl, -jnp.inf), axis=1, keepdims=True)
    i1 = jnp.min(jnp.where(in_grp & (rl == v1), lane, big), axis=1, keepdims=True)
    rest = in_grp & (lane != i1)
    v2 = jnp.max(jnp.where(rest, rl, -jnp.inf), axis=1, keepdims=True)
    i2 = jnp.min(jnp.where(rest & (rl == v2), lane, big), axis=1, keepdims=True)
    e21 = jnp.exp(v2 - v1)
    p1 = g_w / (1.0 + e21)
    p2 = g_w * e21 / (1.0 + e21)
    return i1 - ROUTE_OFF, i2 - ROUTE_OFF, p1, p2


R_E1, R_E2, R_P1, R_P2, R_RANK1, R_RANK2 = range(6)


def _route_kernel(rlp_ref, rls_ref, route_ref, counts_ref, tri_sc, run_sc, *, n_prompt_blocks):
    i = pl.program_id(0)
    tb = rlp_ref.shape[0]

    @pl.when(i == 0)
    def _():
        r = lax.broadcasted_iota(jnp.int32, (tb, tb), 0)
        c = lax.broadcasted_iota(jnp.int32, (tb, tb), 1)
        tri_sc[...] = (r > c).astype(BF16)
        run_sc[...] = jnp.zeros_like(run_sc)

    e1, e2, p1, p2 = _top2_experts(jnp.where(i < n_prompt_blocks, rlp_ref[...], rls_ref[...]))
    lane = lax.broadcasted_iota(jnp.int32, (tb, LANES), 1)
    oh1 = (lane == e1).astype(F32)
    oh2 = (lane == e2).astype(F32)
    both = oh1 + oh2
    before = _dot(tri_sc[...], both.astype(BF16)) + run_sc[...]
    rank1 = jnp.sum(before * oh1, axis=1, keepdims=True)
    rank2 = jnp.sum(before * oh2, axis=1, keepdims=True)
    run_sc[...] += jnp.sum(both, axis=0, keepdims=True)
    rec = jnp.zeros((tb, LANES), F32)
    for k, val in ((R_E1, e1.astype(F32)), (R_E2, e2.astype(F32)), (R_P1, p1), (R_P2, p2),
                   (R_RANK1, rank1), (R_RANK2, rank2)):
        rec = jnp.where(lane == k, val, rec)
    route_ref[...] = rec
    counts_ref[...] = run_sc[...]


def _route_call(rl_p, rl_s):
    tb = TOKEN_BLOCK
    npb = rl_p.shape[0] // tb
    t = rl_p.shape[0] + rl_s.shape[0]
    return pl.pallas_call(
        functools.partial(_route_kernel, n_prompt_blocks=npb),
        grid=(t // tb,),
        in_specs=[pl.BlockSpec((tb, LANES), lambda i: (jnp.minimum(i, npb - 1), 0)),
                  pl.BlockSpec((tb, LANES), lambda i: (0, 0))],
        out_specs=[pl.BlockSpec((tb, LANES), lambda i: (i, 0)), pl.BlockSpec((1, LANES), lambda i: (0, 0))],
        out_shape=[jax.ShapeDtypeStruct((t, LANES), F32), jax.ShapeDtypeStruct((1, LANES), F32)],
        scratch_shapes=[pltpu.VMEM((tb, tb), BF16), pltpu.VMEM((1, LANES), F32)],
        name="moe_route",
        compiler_params=pltpu.CompilerParams(dimension_semantics=("arbitrary",), vmem_limit_bytes=VMEM_LIMIT),
    )(rl_p, rl_s)


def _expert_layout(counts_row, n_tiles):
    tr = EXPERT_ROWS
    counts = counts_row[0, :N_EXPERTS].astype(jnp.int32)
    padded = ((counts + tr - 1) // tr) * tr
    ends = jnp.cumsum(padded)
    starts = ends - padded
    n_used = ends[-1] // tr
    tile_id = jnp.arange(n_tiles, dtype=jnp.int32)
    tile_expert = jnp.sum((jnp.minimum(tile_id, n_used - 1)[:, None] * tr >= ends[None, :]).astype(jnp.int32), axis=1)
    is_last = jnp.any((counts > 0)[None, :] & (tile_id[:, None] == (ends // tr - 1)[None, :]), axis=1)
    zero_tile = (is_last | (tile_id >= n_used)).astype(jnp.int32)
    starts_row = jnp.pad(starts.astype(F32), (0, LANES - N_EXPERTS))[None, :]
    return starts_row, tile_expert, n_used.reshape(1), zero_tile


def _positions_kernel(route_ref, starts_ref, pos_ref):
    rec = route_ref[...]
    lane = lax.broadcasted_iota(jnp.int32, rec.shape, 1)
    starts = starts_ref[...]
    start1 = jnp.sum(jnp.where(lane == rec[:, R_E1:R_E1 + 1].astype(jnp.int32), starts, 0.0), axis=1, keepdims=True)
    start2 = jnp.sum(jnp.where(lane == rec[:, R_E2:R_E2 + 1].astype(jnp.int32), starts, 0.0), axis=1, keepdims=True)
    pos = jnp.where(lane == 0, start1 + rec[:, R_RANK1:R_RANK1 + 1],
                    jnp.where(lane == 1, start2 + rec[:, R_RANK2:R_RANK2 + 1], 0.0))
    pos_ref[...] = pos.T[0:SUBLANES, :].astype(jnp.int32)


def _positions_call(route, starts_row):
    t = route.shape[0]
    tb = TOKEN_BLOCK
    return pl.pallas_call(
        _positions_kernel,
        grid=(t // tb,),
        in_specs=[pl.BlockSpec((tb, LANES), lambda i: (i, 0)), pl.BlockSpec((1, LANES), lambda i: (0, 0))],
        out_specs=pl.BlockSpec((SUBLANES, tb), lambda i: (0, i)),
        out_shape=jax.ShapeDtypeStruct((SUBLANES, t), jnp.int32),
        name="moe_positions",
        compiler_params=pltpu.CompilerParams(dimension_semantics=("arbitrary",)),
    )(route, starts_row)


def _row_copy(src, dst, sem):
    return pltpu.make_async_copy(src, dst, sem)


def _dispatch_kernel(pos1_ref, pos2_ref, zt_ref, hp_ref, hs_ref, xs_hbm, zero_sc, sem, *, n_prompt_blocks):
    i = pl.program_id(0)
    tb = TOKEN_BLOCK
    tr = EXPERT_ROWS

    @pl.when(i == 0)
    def _():
        zero_sc[...] = jnp.zeros_like(zero_sc)

        def ztile(m):
            rows = tr * D_TILES
            return _row_copy(zero_sc, xs_hbm.at[pl.ds(pl.multiple_of(m * rows, rows), rows)], sem)

        def zstart(m, c):
            @pl.when(zt_ref[m] != 0)
            def _():
                ztile(m).start()
            return c

        def zwait(m, c):
            @pl.when(zt_ref[m] != 0)
            def _():
                ztile(m).wait()
            return c

        lax.fori_loop(0, zt_ref.shape[0], zstart, 0)
        lax.fori_loop(0, zt_ref.shape[0], zwait, 0)

    def scatter(src_ref):
        def copies(r):
            t = i * tb + r
            return (_row_copy(src_ref.at[_token_rows(r)], xs_hbm.at[_token_rows(pos1_ref[t])], sem),
                    _row_copy(src_ref.at[_token_rows(r)], xs_hbm.at[_token_rows(pos2_ref[t])], sem))

        def start(r, c):
            a, b = copies(r)
            a.start()
            b.start()
            return c

        def wait(r, c):
            a, b = copies(r)
            a.wait()
            b.wait()
            return c

        lax.fori_loop(0, tb, start, 0, unroll=8)
        lax.fori_loop(0, tb, wait, 0, unroll=8)

    @pl.when(i < n_prompt_blocks)
    def _():
        scatter(hp_ref)

    @pl.when(i >= n_prompt_blocks)
    def _():
        scatter(hs_ref)


def _dispatch_call(pos1, pos2, zero_tile, h1_p, h1_s, n_tiles):
    tb = TOKEN_BLOCK
    npb = h1_p.shape[0] // (tb * D_TILES)
    t = (h1_p.shape[0] + h1_s.shape[0]) // D_TILES
    grid_spec = pltpu.PrefetchScalarGridSpec(
        num_scalar_prefetch=3,
        grid=(t // tb,),
        in_specs=[pl.BlockSpec((tb * D_TILES, LANES), lambda i, *_: (jnp.minimum(i, npb - 1), 0)),
                  pl.BlockSpec((tb * D_TILES, LANES), lambda i, *_: (0, 0))],
        out_specs=pl.BlockSpec(memory_space=pl.ANY),
        scratch_shapes=[pltpu.VMEM((EXPERT_ROWS * D_TILES, LANES), F32), pltpu.SemaphoreType.DMA(())],
    )
    return pl.pallas_call(
        functools.partial(_dispatch_kernel, n_prompt_blocks=npb), grid_spec=grid_spec,
        out_shape=jax.ShapeDtypeStruct((n_tiles * EXPERT_ROWS * D_TILES, LANES), F32), name="moe_dispatch",
        compiler_params=pltpu.CompilerParams(dimension_semantics=("arbitrary",), vmem_limit_bytes=VMEM_LIMIT),
    )(pos1, pos2, zero_tile, h1_p, h1_s)


def _expert_kernel(te_ref, nu_ref, x_ref, wg_ref, wu_ref, wd_ref, y_ref):
    del te_ref
    used = pl.program_id(0) < nu_ref[0]

    @pl.when(used)
    def _():
        x = _from_tiles(x_ref, EXPERT_ROWS).astype(BF16)
        hid = _silu(_dot(x, wg_ref[...])) * _dot(x, wu_ref[...])
        _to_tiles(y_ref, _dot(hid.astype(BF16), wd_ref[...]))

    @pl.when(jnp.logical_not(used))
    def _():
        y_ref[...] = jnp.zeros_like(y_ref)


def _expert_call(tile_expert, n_used, xs, wg, wu, wd, n_tiles):
    tr = EXPERT_ROWS
    rows = lambda i, te, nu: (jnp.minimum(i, nu[0] - 1), 0)
    out_rows = lambda i, te, nu: (i, 0)
    wsel = lambda i, te, nu: (te[i], 0, 0)
    grid_spec = pltpu.PrefetchScalarGridSpec(
        num_scalar_prefetch=2,
        grid=(n_tiles,),
        in_specs=[pl.BlockSpec((tr * D_TILES, LANES), rows),
                  pl.BlockSpec((None, D_MODEL, D_EXPERT), wsel),
                  pl.BlockSpec((None, D_MODEL, D_EXPERT), wsel),
                  pl.BlockSpec((None, D_EXPERT, D_MODEL), wsel)],
        out_specs=pl.BlockSpec((tr * D_TILES, LANES), out_rows),
    )
    return pl.pallas_call(
        _expert_kernel, grid_spec=grid_spec,
        out_shape=jax.ShapeDtypeStruct((n_tiles * tr * D_TILES, LANES), F32), name="moe_experts",
        compiler_params=pltpu.CompilerParams(dimension_semantics=("arbitrary",), vmem_limit_bytes=VMEM_LIMIT),
    )(tile_expert, n_used, xs, wg, wu, wd)


def _combine_kernel(pos1_ref, pos2_ref, hp_ref, hs_ref, route_ref, g2_ref, b2_ref, ys_hbm, op_ref, os_ref, ybuf, sem,
                    *, n_prompt_blocks):
    i = pl.program_id(0)
    n = pl.num_programs(0)
    tb = TOKEN_BLOCK

    def copies(blk, slot, r):
        t = blk * tb + r
        return (_row_copy(ys_hbm.at[_token_rows(pos1_ref[t])], ybuf.at[slot, 0, _token_rows(r)], sem.at[slot]),
                _row_copy(ys_hbm.at[_token_rows(pos2_ref[t])], ybuf.at[slot, 1, _token_rows(r)], sem.at[slot]))

    def issue(blk, slot):
        def body(r, c):
            a, b = copies(blk, slot, r)
            a.start()
            b.start()
            return c
        lax.fori_loop(0, tb, body, 0, unroll=8)

    def drain(blk, slot):
        def body(r, c):
            a, b = copies(blk, slot, r)
            a.wait()
            b.wait()
            return c
        lax.fori_loop(0, tb, body, 0, unroll=8)

    @pl.when(i == 0)
    def _():
        issue(0, 0)

    @pl.when(i + 1 < n)
    def _():
        issue(i + 1, (i + 1) % 2)

    slot = i % 2
    drain(i, slot)
    route = route_ref[...]
    p1 = route[:, R_P1:R_P1 + 1]
    p2 = route[:, R_P2:R_P2 + 1]
    h = jnp.where(i < n_prompt_blocks, _from_tiles(hp_ref, tb), _from_tiles(hs_ref, tb))
    moe = p1 * _from_tiles(ybuf.at[slot, 0], tb) + p2 * _from_tiles(ybuf.at[slot, 1], tb)
    out = _layer_norm(ALPHA * h + moe, g2_ref[...], b2_ref[...])

    @pl.when(i < n_prompt_blocks)
    def _():
        op_ref[...] = out

    @pl.when(i >= n_prompt_blocks)
    def _():
        os_ref[...] = out


def _combine_call(pos1, pos2, h1_p, h1_s, route, g2, b2, ys):
    n_prompt_tokens = h1_p.shape[0] // D_TILES
    t = n_prompt_tokens + h1_s.shape[0] // D_TILES
    tb = TOKEN_BLOCK
    npb = n_prompt_tokens // tb
    grid_spec = pltpu.PrefetchScalarGridSpec(
        num_scalar_prefetch=2,
        grid=(t // tb,),
        in_specs=[pl.BlockSpec((tb * D_TILES, LANES), lambda i, *_: (jnp.minimum(i, npb - 1), 0)),
                  pl.BlockSpec((tb * D_TILES, LANES), lambda i, *_: (0, 0)),
                  pl.BlockSpec((tb, LANES), lambda i, *_: (i, 0)),
                  pl.BlockSpec((1, D_MODEL), lambda i, *_: (0, 0)),
                  pl.BlockSpec((1, D_MODEL), lambda i, *_: (0, 0)),
                  pl.BlockSpec(memory_space=pl.ANY)],
        out_specs=[pl.BlockSpec((tb, D_MODEL), lambda i, *_: (jnp.minimum(i, npb - 1), 0)),
                   pl.BlockSpec((tb, D_MODEL), lambda i, *_: (0, 0))],
        scratch_shapes=[pltpu.VMEM((2, TOP_K, tb * D_TILES, LANES), F32), pltpu.SemaphoreType.DMA((2,))],
    )
    return pl.pallas_call(
        functools.partial(_combine_kernel, n_prompt_blocks=npb), grid_spec=grid_spec,
        out_shape=[jax.ShapeDtypeStruct((n_prompt_tokens, D_MODEL), F32),
                   jax.ShapeDtypeStruct((tb, D_MODEL), F32)],
        name="moe_combine",
        compiler_params=pltpu.CompilerParams(dimension_semantics=("arbitrary",), vmem_limit_bytes=VMEM_LIMIT),
    )(pos1, pos2, h1_p, h1_s, route, g2, b2, ys)


def _rope_tables(pos):
    inv_freq = jnp.exp(-math.log(ROPE_THETA) * jnp.arange(HALF, dtype=F32) * (2.0 / HEAD_DIM))
    ang = pos.astype(F32)[:, None] * inv_freq[None, :]
    cos = jnp.cos(ang)
    sin = jnp.sin(ang)
    cos_t = jnp.tile(jnp.concatenate([cos, cos], axis=1), (1, LANES // HEAD_DIM))
    sin_t = jnp.tile(jnp.concatenate([-sin, sin], axis=1), (1, LANES // HEAD_DIM))
    return cos_t, sin_t


def _pad_lanes(a, n=LANES):
    return jnp.pad(a, ((0, 0), (0, n - a.shape[1])))


def kernel(x_prompt, x_sample, cache_meta_k, cache_meta_v, cache_win_k, cache_win_v, state_conv, state_ssm,
           meta_tokens, w_in, attn_sinks, conv_w, conv_b, dt_bias, a_log, d_skip, ssm_norm_w, w_o,
           ln1_g, ln1_b, router_group_w, router_group_b, router_expert_w, router_expert_b,
           w_gate, w_up, w_down, ln2_g, ln2_b):
    B, S, D = x_prompt.shape
    N = x_sample.shape[0]
    l = 0

    wi = w_in[l]
    dt0 = C_GA
    w_in_r = jnp.concatenate([wi[:, :dt0], wi[:, dt0 + SSM_HEADS:], wi[:, dt0:dt0 + SSM_HEADS],
                              jnp.zeros((D, LANES - SSM_HEADS), wi.dtype)], axis=1).astype(BF16)
    w_o_b = w_o[l].astype(BF16)
    wr = _pad_lanes(jnp.concatenate([router_group_w[l], router_expert_w[l]], axis=1))
    wr_hi = wr.astype(BF16)
    wr_lo = (wr - wr_hi.astype(F32)).astype(BF16)
    br = _pad_lanes(jnp.concatenate([router_group_b[l], router_expert_b[l]])[None, :])
    dtb = _pad_lanes(dt_bias[l][None, :])
    alog = _pad_lanes(a_log[l][None, :])
    dskip_e = jnp.repeat(d_skip[l], SSM_HEAD_DIM)[None, :]
    nw = ssm_norm_w[l][None, :]
    g1, b1 = ln1_g[l][None, :], ln1_b[l][None, :]
    g2, b2 = ln2_g[l][None, :], ln2_b[l][None, :]
    cw_, cb_ = conv_w[l], conv_b[l][None, :]
    sinks = attn_sinks[l]
    wg_b = w_gate[l].astype(BF16)
    wu_b = w_up[l].astype(BF16)
    wd_b = w_down[l].astype(BF16)

    nb = S // BLOCK + 1
    pos_p = jnp.arange(nb * BLOCK, dtype=jnp.int32) - META_PAD
    cos_p, sin_p = _rope_tables(pos_p)
    meta_blk = jnp.concatenate([jnp.zeros((META_PAD, D), F32), meta_tokens.astype(F32)], axis=0)
    (h1_p, rl_p, p_mk, p_mv, p_wk, p_wv, p_conv, p_ssm) = _prompt_call(
        x_prompt, meta_blk, w_in_r, w_o_b, cos_p, sin_p, cw_, cb_, dtb, alog, dskip_e, nw, g1, b1,
        wr_hi, wr_lo, br, sinks)

    xs2d = x_sample.reshape(N, D)
    cos_s, sin_s = _rope_tables(jnp.full((1,), PAST_LEN, jnp.int32))
    sc = state_conv[l]
    expand = (jnp.arange(LANES, dtype=jnp.int32)[:, None]
              == (jnp.arange(SSM_INNER, dtype=jnp.int32) // SSM_HEAD_DIM)[None, :]).astype(BF16)
    (q_s, k_s, v_s, z_s, xraw_s, xdt_s, dec_s, xsc_s, b_s, c_s, ga_s, gs_s) = _sample_proj_call(
        xs2d, w_in_r, cos_s, sin_s, sc[:, 0], sc[:, 1], sc[:, 2], cw_, cb_, dtb, alog, expand)
    head_group = jnp.arange(N_HEADS, dtype=jnp.int32) // Q_PER_KV
    grp_mask = (head_group[:, None] == jnp.arange(N_KV_HEADS, dtype=jnp.int32)[None, :]).astype(F32)
    qblk = (q_s.reshape(N, N_HEADS, 1, HEAD_DIM) * grp_mask[None, :, :, None]).reshape(N, N_HEADS, KV_DIM)
    wlen = cache_win_k.shape[2]
    o_blk = _sample_attn_call(
        qblk, cache_meta_k[l].reshape(N, N_META, KV_DIM), cache_meta_v[l].reshape(N, N_META, KV_DIM),
        cache_win_k[l].reshape(N, wlen, KV_DIM), cache_win_v[l].reshape(N, wlen, KV_DIM),
        k_s.reshape(N, 1, KV_DIM), v_s.reshape(N, 1, KV_DIM), sinks.reshape(1, N_HEADS, 1), 16)
    o4 = o_blk.reshape(N, N_HEADS, N_KV_HEADS, HEAD_DIM)
    y_attn_s = jnp.sum(o4 * grp_mask[None, :, :, None], axis=2).reshape(N, Q_DIM)
    s_ssm3, h1_s, rl_s = _sample_ssd_call(
        xdt_s, dec_s, b_s, c_s, state_ssm[l].reshape(N, SSM_INNER, SSM_STATE), y_attn_s, xsc_s, z_s, ga_s, gs_s,
        xs2d, dskip_e, nw, w_o_b, g1, b1, wr_hi, wr_lo, br, 8)

    n_tok = rl_p.shape[0] + rl_s.shape[0]
    n_tiles = (TOP_K * n_tok) // EXPERT_ROWS + N_EXPERTS
    route, counts = _route_call(rl_p, rl_s)
    starts_row, tile_expert, n_used, zero_tile = _expert_layout(counts, n_tiles)
    pos = _positions_call(route, starts_row)
    pos1, pos2 = pos[0], pos[1]
    xs_sorted = _dispatch_call(pos1, pos2, zero_tile, h1_p, h1_s, n_tiles)
    ys_sorted = _expert_call(tile_expert, n_used, xs_sorted, wg_b, wu_b, wd_b, n_tiles)
    y_p2d, y_s2d = _combine_call(pos1, pos2, h1_p, h1_s, route, g2, b2, ys_sorted)
    y_prompt = y_p2d.reshape(B, S, D)
    y_sample = y_s2d[:N].reshape(N, 1, D)

    kv5 = lambda a, n, t: a.reshape(1, n, t, N_KV_HEADS, HEAD_DIM)
    s_conv = jnp.stack([sc[:, 1], sc[:, 2], xraw_s], axis=1)[None]
    return (y_prompt, y_sample,
            kv5(p_mk, B, N_META), kv5(p_mv, B, N_META), kv5(p_wk, B, WINDOW), kv5(p_wv, B, WINDOW),
            p_conv[None], p_ssm.reshape(1, B, SSM_HEADS, SSM_HEAD_DIM, SSM_STATE),
            kv5(k_s, N, 1), kv5(v_s, N, 1), s_conv,
            s_ssm3.reshape(1, N, SSM_HEADS, SSM_HEAD_DIM, SSM_STATE))
```

```python
import functools
import math

import jax
import jax.numpy as jnp
from jax import lax
from jax.experimental import pallas as pl
from jax.experimental.pallas import tpu as pltpu

F32 = jnp.float32
BF16 = jnp.bfloat16

D_MODEL = 1024
N_META = 16
N_HEADS = 16
N_KV_HEADS = 4
HEAD_DIM = 64
HALF = HEAD_DIM // 2
Q_PER_KV = N_HEADS // N_KV_HEADS
WINDOW = 128
BLOCK = 128
META_PAD = BLOCK - N_META
ROPE_THETA = 10000.0
PAST_LEN = 16384
SSM_INNER = D_MODEL
SSM_HEAD_DIM = 64
SSM_HEADS = SSM_INNER // SSM_HEAD_DIM
SSM_GROUPS = 4
SSM_STATE = 128
CONV_WIDTH = 4
GN = SSM_GROUPS * SSM_STATE
CONV_DIM = SSM_INNER + 2 * GN
Q_DIM = N_HEADS * HEAD_DIM
KV_DIM = N_KV_HEADS * HEAD_DIM
N_EXPERT_GROUPS = 4
EXPERTS_PER_GROUP = 8
N_EXPERTS = N_EXPERT_GROUPS * EXPERTS_PER_GROUP
D_EXPERT = 256
DEPTH = 1
ALPHA = (2 * DEPTH) ** 0.25
LN_EPS = 1e-5
RMS_EPS = 1e-5
NEG_INF = -1e30

LANES = 128
SUBLANES = 8
VMEM_LIMIT = 56 * 1024 * 1024

C_Q = 0
C_K = C_Q + Q_DIM
C_V = C_K + KV_DIM
C_Z = C_V + KV_DIM
C_XBC = C_Z + SSM_INNER
C_GA = C_XBC + CONV_DIM
C_GS = C_GA + D_MODEL
C_DT = C_GS + D_MODEL
IN_PAD = C_DT + LANES
ROUTE_OFF = N_EXPERT_GROUPS
D_TILES = D_MODEL // LANES
TOKEN_BLOCK = 512
EXPERT_ROWS = 256
TOP_K = 2


def _to_tiles(ref, x):
    n = x.shape[0]
    for c in range(D_TILES):
        ref[pl.ds(c, n, stride=D_TILES), :] = x[:, c * LANES:(c + 1) * LANES]


def _from_tiles(ref, n):
    return jnp.concatenate([ref[pl.ds(c, n, stride=D_TILES), :] for c in range(D_TILES)], axis=1)


def _token_rows(t):
    return pl.ds(pl.multiple_of(t * D_TILES, D_TILES), D_TILES)


def _const_spec(shape):
    nd = len(shape)
    return pl.BlockSpec(shape, lambda *_: (0,) * nd, pipeline_mode=pl.Buffered(1))


def _dot(a, b):
    return jnp.dot(a, b, preferred_element_type=F32)


def _dot_nt(a, b):
    return lax.dot_general(a, b, (((1,), (1,)), ((), ())), preferred_element_type=F32)


def _dot_tn(a, b):
    return lax.dot_general(a, b, (((0,), (0,)), ((), ())), preferred_element_type=F32)


def _split3(x):
    p1 = x.astype(BF16)
    r1 = x - p1.astype(F32)
    p2 = r1.astype(BF16)
    r2 = r1 - p2.astype(F32)
    return p1, p2, r2.astype(BF16)


def _dot_exact_rhs(x, m_bf16):
    p1, p2, p3 = _split3(x)
    return _dot(p1, m_bf16) + _dot(p2, m_bf16) + _dot(p3, m_bf16)


def _dot_exact_lhs(m_bf16, x):
    p1, p2, p3 = _split3(x)
    return _dot(m_bf16, p1) + _dot(m_bf16, p2) + _dot(m_bf16, p3)


def _silu(x):
    return x * jax.nn.sigmoid(x)


def _softplus(x):
    return jnp.maximum(x, 0.0) + jnp.log1p(jnp.exp(-jnp.abs(x)))


def _layer_norm(x, g, b):
    mu = jnp.mean(x, -1, keepdims=True)
    xc = x - mu
    var = jnp.mean(xc * xc, -1, keepdims=True)
    return xc * lax.rsqrt(var + LN_EPS) * g + b


def _rope_tiles(t, cos, sin_signed):
    lane = lax.broadcasted_iota(jnp.int32, (t.shape[0], LANES), 1)
    first_half = (lane % HEAD_DIM) < HALF
    outs = []
    for i in range(t.shape[1] // LANES):
        x = t[:, i * LANES:(i + 1) * LANES]
        partner = jnp.where(first_half, pltpu.roll(x, LANES - HALF, 1), pltpu.roll(x, HALF, 1))
        outs.append(x * cos + partner * sin_signed)
    return jnp.concatenate(outs, axis=1) if len(outs) > 1 else outs[0]


def _gated_rmsnorm(y, z, nw):
    yf = y * _silu(z)
    gw = SSM_INNER // SSM_GROUPS
    outs = []
    for g in range(SSM_GROUPS):
        t = yf[:, g * gw:(g + 1) * gw]
        ms = jnp.mean(t * t, -1, keepdims=True)
        outs.append(t * lax.rsqrt(ms + RMS_EPS))
    return jnp.concatenate(outs, axis=1) * nw


def _router_logits(h, wr_hi, wr_lo, br):
    h_hi = h.astype(BF16)
    h_lo = (h - h_hi.astype(F32)).astype(BF16)
    return _dot(h_hi, wr_hi) + _dot(h_lo, wr_hi) + _dot(h_hi, wr_lo) + br


def _prompt_kernel(sinks_ref, x_ref, meta_ref, bias_ref, win_ref, wo_ref, cos_ref, sin_ref, convw_ref, convb_ref,
                   dtb_ref, alog_ref, dskip_ref, nw_ref, g1_ref, b1_ref, wrh_ref, wrl_ref, br_ref,
                   h1_ref, rl_ref, kmeta_ref, vmeta_ref, kwin_ref, vwin_ref, conv_ref, ssm_ref,
                   kd_sc, vd_sc, xbuf_sc, q_st, kd_st, vd_st, z_st, xraw_st, dt_st, ga_st, gs_st, x_st):
    j = pl.program_id(1)
    jc = j - 1
    sp, sc = 1, 0
    T = BLOCK
    st_sc = ssm_ref
    staging = (q_st, kd_st, vd_st, z_st, xraw_st, dt_st, ga_st, gs_st, x_st)

    @pl.when(j == 0)
    def _():
        kd_sc[...] = jnp.zeros_like(kd_sc)
        vd_sc[...] = jnp.zeros_like(vd_sc)
        xbuf_sc[0:SUBLANES, :] = jnp.zeros((SUBLANES, CONV_DIM), F32)
        st_sc[...] = jnp.zeros_like(st_sc)
        for ref in staging:
            ref[...] = jnp.zeros_like(ref)

    lane = lax.broadcasted_iota(jnp.int32, (T, LANES), 1)
    row = lax.broadcasted_iota(jnp.int32, (T, LANES), 0)
    lo = lane < HEAD_DIM

    x_new = jnp.where(j == 0, meta_ref[...], x_ref[...])
    xb = x_new.astype(BF16)
    cos = cos_ref[...]
    sin = sin_ref[...]
    def dup(t, g):
        tile = t[:, (g // 2) * LANES:(g // 2 + 1) * LANES]
        rolled = pltpu.roll(tile, HEAD_DIM, 1)
        return jnp.where(lo if g % 2 == 0 else jnp.logical_not(lo), tile, rolled)

    W2 = 2 * LANES

    def proj(col, width=W2):
        return _dot(xb, win_ref[:, col:col + width])

    def task_q(t):
        def run():
            q = _rope_tiles(proj(C_Q + t * W2), cos, sin)
            q_st[sp, :, t * W2:(t + 1) * W2] = (q * (HEAD_DIM ** -0.5)).astype(BF16)
        return run

    def task_k():
        k = _rope_tiles(proj(C_K), cos, sin)
        kwin_ref[...] = k
        kd_st[sp] = jnp.concatenate([dup(k, g) for g in range(N_KV_HEADS)], axis=1).astype(BF16)

    def task_v():
        v = proj(C_V)
        vwin_ref[...] = v
        vd_st[sp] = jnp.concatenate([dup(v, g) for g in range(N_KV_HEADS)], axis=1).astype(BF16)

    def task_plain(ref, col0, t):
        def run():
            ref[sp, :, t * W2:(t + 1) * W2] = proj(col0 + t * W2)
        return run

    def task_tail():
        dt_st[sp] = proj(C_DT, LANES)
        x_st[sp] = x_new

    attn_tasks = ([task_k, task_v] + [task_q(t) for t in range(Q_DIM // W2)]
                  + [task_plain(z_st, C_Z, t) for t in range(SSM_INNER // W2)])
    ssd_tasks = ([task_plain(xraw_st, C_XBC, t) for t in range(CONV_DIM // W2)]
                 + [task_plain(ga_st, C_GA, t) for t in range(D_MODEL // W2)]
                 + [task_plain(gs_st, C_GS, t) for t in range(D_MODEL // W2)])

    def run_share(tasks, part, parts):
        n = len(tasks)
        for t in tasks[part * n // parts:(part + 1) * n // parts]:
            t()

    kd_cur = kd_st[sc]
    vd_cur = vd_st[sc]
    kd_sc[2 * T:3 * T, :] = kd_cur
    vd_sc[2 * T:3 * T, :] = vd_cur

    kd_sc[0:T, :] = jnp.where(jc == 0, kd_cur, kd_sc[0:T, :])
    vd_sc[0:T, :] = jnp.where(jc == 0, vd_cur, vd_sc[0:T, :])

    nq = Q_PER_KV * T
    bias = bias_ref[...]
    rcol = lax.broadcasted_iota(jnp.int32, (nq, 1), 0)

    attn_tiles = []
    for g in range(N_KV_HEADS):
        run_share(attn_tasks, g, N_KV_HEADS)
        qt0 = q_st[sc, :, (2 * g) * LANES:(2 * g + 1) * LANES]
        qt1 = q_st[sc, :, (2 * g + 1) * LANES:(2 * g + 2) * LANES]
        zero = jnp.zeros_like(qt0)
        lhs = jnp.concatenate([jnp.where(lo, qt0, zero), jnp.where(lo, zero, qt0),
                               jnp.where(lo, qt1, zero), jnp.where(lo, zero, qt1)], axis=0)
        s = _dot_nt(lhs, kd_sc[:, g * LANES:(g + 1) * LANES])
        s = (s.reshape(Q_PER_KV, T, 3 * T) + bias[None]).reshape(nq, 3 * T)
        sink = jnp.where(rcol < T, sinks_ref[4 * g],
                         jnp.where(rcol < 2 * T, sinks_ref[4 * g + 1],
                                   jnp.where(rcol < 3 * T, sinks_ref[4 * g + 2], sinks_ref[4 * g + 3])))
        m = jnp.maximum(jnp.max(s, axis=1, keepdims=True), sink)
        p = jnp.exp(s - m)
        den = jnp.sum(p, axis=1, keepdims=True) + jnp.exp(sink - m)
        o = _dot(p.astype(BF16), vd_sc[:, g * LANES:(g + 1) * LANES]) / den
        attn_tiles.append(jnp.where(lo, o[0:T], o[T:2 * T]))
        attn_tiles.append(jnp.where(lo, o[2 * T:3 * T], o[3 * T:4 * T]))
    y_attn = jnp.concatenate(attn_tiles, axis=1)

    kd_sc[T:2 * T, :] = kd_cur
    vd_sc[T:2 * T, :] = vd_cur

    z = z_st[sc]
    xbuf_sc[SUBLANES:SUBLANES + T, :] = xraw_st[sc]
    acc = convb_ref[...]
    for kk in range(CONV_WIDTH):
        acc = acc + xbuf_sc[SUBLANES - kk:SUBLANES - kk + T, :] * convw_ref[CONV_WIDTH - 1 - kk:CONV_WIDTH - kk, :]

    conv_ref[...] = xbuf_sc[SUBLANES + T - (CONV_WIDTH - 1):SUBLANES + T, :]
    xbuf_sc[0:SUBLANES, :] = xbuf_sc[T:T + SUBLANES, :]
    xc = _silu(acc)
    xs = xc[:, :SSM_INNER]
    bm = xc[:, SSM_INNER:SSM_INNER + GN].astype(BF16)
    cm = xc[:, SSM_INNER + GN:].astype(BF16)

    valid = (jc * T + row - META_PAD) >= 0
    dt = jnp.where(valid, _softplus(dt_st[sc] + dtb_ref[...]), 0.0)
    ad = dt * (-jnp.exp(alog_ref[...]))
    tril = row >= lane
    cs = _dot_exact_lhs(tril.astype(BF16), ad)
    cs_t = cs.T

    hp_row = lax.broadcasted_iota(jnp.int32, (LANES, LANES), 0) < SSM_HEAD_DIM
    y_tiles = []
    for i in range(SSM_HEADS // 2):
        run_share(ssd_tasks, i, SSM_HEADS // 2)
        g = i // 2
        h0, h1 = 2 * i, 2 * i + 1
        bg = bm[:, g * SSM_STATE:(g + 1) * SSM_STATE]
        cg = cm[:, g * SSM_STATE:(g + 1) * SSM_STATE]
        if i % 2 == 0:
            cb = _dot_nt(cg, bg)
        col0, col1 = cs[:, h0:h0 + 1], cs[:, h1:h1 + 1]
        row0, row1 = cs_t[h0:h0 + 1, :], cs_t[h1:h1 + 1, :]
        tot0, tot1 = cs_t[h0:h0 + 1, T - 1:T], cs_t[h1:h1 + 1, T - 1:T]
        m0 = (cb * jnp.exp(jnp.where(tril, col0 - row0, NEG_INF))).astype(BF16)
        m1 = (cb * jnp.exp(jnp.where(tril, col1 - row1, NEG_INF))).astype(BF16)
        xs_i = xs[:, i * LANES:(i + 1) * LANES]
        xd = xs_i * jnp.where(lo, dt[:, h0:h0 + 1], dt[:, h1:h1 + 1])
        xdb = xd.astype(BF16)
        y_diag = jnp.where(lo, _dot(m0, xdb), _dot(m1, xdb))
        s_pair = st_sc[i * LANES:(i + 1) * LANES, :]
        y_off = _dot_nt(cg, s_pair.astype(BF16)) * jnp.where(lo, jnp.exp(col0), jnp.exp(col1))
        wgt = jnp.where(lo, jnp.exp(tot0 - col0), jnp.exp(tot1 - col1))
        upd = _dot_tn((xd * wgt).astype(BF16), bg)
        st_sc[i * LANES:(i + 1) * LANES, :] = jnp.where(hp_row, jnp.exp(tot0), jnp.exp(tot1)) * s_pair + upd
        y_tiles.append(y_diag + y_off + dskip_ref[:, i * LANES:(i + 1) * LANES] * xs_i)
    task_tail()
    y = jnp.concatenate(y_tiles, axis=1)
    y_ssd = _gated_rmsnorm(y, z, nw_ref[...])

    mixin = jax.nn.sigmoid(ga_st[sc]) * y_attn + jax.nn.sigmoid(gs_st[sc]) * y_ssd
    mix = _dot(mixin.astype(BF16), wo_ref[...])
    h1 = _layer_norm(ALPHA * x_st[sc] + mix, g1_ref[...], b1_ref[...])
    _to_tiles(h1_ref, h1)
    rl_ref[...] = _router_logits(h1, wrh_ref[...], wrl_ref[...], br_ref[...])
    for ref in staging:
        ref[sc] = ref[sp]

    @pl.when(j == 0)
    def _():
        kmeta_ref[...] = kwin_ref[META_PAD:, :]
        vmeta_ref[...] = vwin_ref[META_PAD:, :]


def _attention_bias():
    t = BLOCK
    rr = jnp.arange(t, dtype=jnp.int32)[:, None]
    cc = jnp.arange(3 * t, dtype=jnp.int32)[None, :]
    ss, kb = cc % t, cc // t
    variants = []
    for jc in (-1, 0, 1, 2):
        mask = (((kb == 0) & (ss >= META_PAD) & ((jc >= 1) | ((jc == 0) & (ss <= rr))))
                | ((kb == 1) & (ss > rr) & (jc >= 2))
                | ((kb == 2) & (ss <= rr) & (jc >= 1)))
        variants.append(jnp.where(mask, 0.0, NEG_INF).astype(F32))
    return jnp.stack(variants)


def _prompt_call(x_prompt, meta_blk, w_in_r, w_o_b, cos_t, sin_t, conv_w, conv_b, dtb, alog, dskip_e, nw,
                 g1, b1, wr_hi, wr_lo, br, sinks):
    B, S, D = x_prompt.shape
    nbs = S // BLOCK
    nb = nbs + 1
    tok = lambda b, j, *_: (b, jnp.clip(j - 1, 0, nbs - 1), 0)
    flat2 = lambda b, j, *_: (b * nbs + jnp.clip(j - 2, 0, nbs - 1), 0)
    tab = lambda b, j, *_: (jnp.minimum(j, nb - 1), 0)
    perb = lambda b, j, *_: (b, 0, 0)
    staged = lambda width, dtype: pltpu.VMEM((2, BLOCK, width), dtype)
    grid_spec = pltpu.PrefetchScalarGridSpec(
        num_scalar_prefetch=1,
        grid=(B, nb + 1),
        in_specs=[
            pl.BlockSpec((None, BLOCK, D), tok),
            _const_spec((BLOCK, D)),
            pl.BlockSpec((None, BLOCK, 3 * BLOCK), lambda b, j, *_: (jnp.minimum(j, 3), 0, 0)),
            _const_spec((D, IN_PAD)),
            _const_spec((D, D)),
            pl.BlockSpec((BLOCK, LANES), tab),
            pl.BlockSpec((BLOCK, LANES), tab),
            _const_spec((CONV_WIDTH, CONV_DIM)),
            _const_spec((1, CONV_DIM)),
            _const_spec((1, LANES)),
            _const_spec((1, LANES)),
            _const_spec((1, SSM_INNER)),
            _const_spec((1, SSM_INNER)),
            _const_spec((1, D)),
            _const_spec((1, D)),
            _const_spec((D, LANES)),
            _const_spec((D, LANES)),
            _const_spec((1, LANES)),
        ],
        out_specs=[
            pl.BlockSpec((BLOCK * D_TILES, LANES), flat2),
            pl.BlockSpec((BLOCK, LANES), flat2),
            pl.BlockSpec((None, N_META, KV_DIM), perb),
            pl.BlockSpec((None, N_META, KV_DIM), perb),
            pl.BlockSpec((None, WINDOW, KV_DIM), perb),
            pl.BlockSpec((None, WINDOW, KV_DIM), perb),
            pl.BlockSpec((None, CONV_WIDTH - 1, CONV_DIM), perb),
            pl.BlockSpec((None, SSM_INNER, SSM_STATE), perb),
        ],
        scratch_shapes=[
            pltpu.VMEM((3 * BLOCK, N_KV_HEADS * LANES), BF16),
            pltpu.VMEM((3 * BLOCK, N_KV_HEADS * LANES), BF16),
            pltpu.VMEM((SUBLANES + BLOCK, CONV_DIM), F32),
            staged(Q_DIM, BF16), staged(N_KV_HEADS * LANES, BF16), staged(N_KV_HEADS * LANES, BF16),
            staged(SSM_INNER, F32), staged(CONV_DIM, F32), staged(LANES, F32),
            staged(D, F32), staged(D, F32), staged(D, F32),
        ],
    )
    n_tok = B * S
    out_shape = [
        jax.ShapeDtypeStruct((n_tok * D_TILES, LANES), F32),
        jax.ShapeDtypeStruct((n_tok, LANES), F32),
        jax.ShapeDtypeStruct((B, N_META, KV_DIM), F32),
        jax.ShapeDtypeStruct((B, N_META, KV_DIM), F32),
        jax.ShapeDtypeStruct((B, WINDOW, KV_DIM), F32),
        jax.ShapeDtypeStruct((B, WINDOW, KV_DIM), F32),
        jax.ShapeDtypeStruct((B, CONV_WIDTH - 1, CONV_DIM), F32),
        jax.ShapeDtypeStruct((B, SSM_INNER, SSM_STATE), F32),
    ]
    return pl.pallas_call(
        _prompt_kernel, grid_spec=grid_spec, out_shape=out_shape, name="prompt_mixer",
        compiler_params=pltpu.CompilerParams(dimension_semantics=("arbitrary", "arbitrary"),
                                             vmem_limit_bytes=VMEM_LIMIT),
    )(sinks, x_prompt, meta_blk, _attention_bias(), w_in_r, w_o_b, cos_t, sin_t, conv_w, conv_b, dtb, alog,
      dskip_e, nw, g1, b1, wr_hi, wr_lo, br)


def _sample_proj_kernel(x_ref, win_ref, cos_ref, sin_ref, h0_ref, h1_ref, h2_ref, convw_ref, convb_ref,
                        dtb_ref, alog_ref, expand_ref,
                        q_ref, k_ref, v_ref, z_ref, xraw_ref, xdt_ref, dec_ref, xs_ref, b_ref, c_ref, ga_ref, gs_ref):
    xb = x_ref[...].astype(BF16)
    cos = cos_ref[...]
    sin = sin_ref[...]
    q_ref[...] = _rope_tiles(_dot(xb, win_ref[:, C_Q:C_K]), cos, sin) * (HEAD_DIM ** -0.5)
    k_ref[...] = _rope_tiles(_dot(xb, win_ref[:, C_K:C_V]), cos, sin)
    v_ref[...] = _dot(xb, win_ref[:, C_V:C_Z])
    z_ref[...] = _dot(xb, win_ref[:, C_Z:C_XBC])
    xraw = _dot(xb, win_ref[:, C_XBC:C_GA])
    xraw_ref[...] = xraw
    acc = (convb_ref[...] + h0_ref[...] * convw_ref[0:1, :] + h1_ref[...] * convw_ref[1:2, :]
           + h2_ref[...] * convw_ref[2:3, :] + xraw * convw_ref[3:4, :])
    xc = _silu(acc)
    xs = xc[:, :SSM_INNER]
    dt = _softplus(_dot(xb, win_ref[:, C_DT:IN_PAD]) + dtb_ref[...])
    ad = dt * (-jnp.exp(alog_ref[...]))
    expand = expand_ref[...]
    dt_e = _dot_exact_rhs(dt, expand)
    ad_e = _dot_exact_rhs(ad, expand)
    xs_ref[...] = xs
    xdt_ref[...] = xs * dt_e
    dec_ref[...] = jnp.exp(ad_e)
    b_ref[...] = xc[:, SSM_INNER:SSM_INNER + GN]
    c_ref[...] = xc[:, SSM_INNER + GN:]
    ga_ref[...] = _dot(xb, win_ref[:, C_GA:C_GS])
    gs_ref[...] = _dot(xb, win_ref[:, C_GS:C_DT])


def _sample_proj_call(xs2d, w_in_r, cos_s, sin_s, hist0, hist1, hist2, conv_w, conv_b, dtb, alog, expand):
    n = xs2d.shape[0]
    shapes = [(n, Q_DIM), (n, KV_DIM), (n, KV_DIM), (n, SSM_INNER), (n, CONV_DIM), (n, SSM_INNER), (n, SSM_INNER),
              (n, SSM_INNER), (n, GN), (n, GN), (n, D_MODEL), (n, D_MODEL)]
    return pl.pallas_call(
        _sample_proj_kernel,
        out_shape=[jax.ShapeDtypeStruct(s, F32) for s in shapes], name="sample_proj",
        compiler_params=pltpu.CompilerParams(vmem_limit_bytes=VMEM_LIMIT),
    )(xs2d, w_in_r, cos_s, sin_s, hist0, hist1, hist2, conv_w, conv_b, dtb, alog, expand)


def _sample_attn_kernel(qblk_ref, mk_ref, mv_ref, wk_ref, wv_ref, nk_ref, nv_ref, sink_ref, o_ref):
    qb = qblk_ref[...]
    qbb = qb.astype(BF16)
    bdot = lambda a, b, dims: lax.dot_general(a, b, dims, preferred_element_type=F32)
    nt = (((2,), (2,)), ((0,), (0,)))
    nn = (((2,), (1,)), ((0,), (0,)))
    wk = wk_ref[...].astype(BF16)
    wv = wv_ref[...].astype(BF16)
    mk = mk_ref[...].astype(BF16)
    mv = mv_ref[...].astype(BF16)
    nk = nk_ref[...]
    nv = nv_ref[...]
    s_win = bdot(qbb, wk, nt)
    s_meta = bdot(qbb, mk, nt)
    s_new = jnp.sum(qb * nk, axis=2, keepdims=True)
    w = wk_ref.shape[1]
    slot = lax.broadcasted_iota(jnp.int32, s_win.shape, 2)
    s_win = jnp.where((w - slot) < WINDOW, s_win, NEG_INF)
    sink = sink_ref[...]
    m = jnp.maximum(jnp.maximum(jnp.max(s_win, axis=2, keepdims=True), jnp.max(s_meta, axis=2, keepdims=True)),
                    jnp.maximum(s_new, sink))
    p_win = jnp.exp(s_win - m)
    p_meta = jnp.exp(s_meta - m)
    p_new = jnp.exp(s_new - m)
    den = (jnp.sum(p_win, axis=2, keepdims=True) + jnp.sum(p_meta, axis=2, keepdims=True) + p_new
           + jnp.exp(sink - m))
    o = (bdot(p_win.astype(BF16), wv, nn) + bdot(p_meta.astype(BF16), mv, nn)
         + p_new * nv)
    o_ref[...] = o / den


def _sample_attn_call(qblk, mk, mv, wk, wv, nk, nv, sink3, nblk):
    n = qblk.shape[0]
    w = wk.shape[1]
    b3 = lambda shape: pl.BlockSpec(shape, lambda i: (i, 0, 0))
    return pl.pallas_call(
        _sample_attn_kernel,
        grid=(n // nblk,),
        in_specs=[b3((nblk, N_HEADS, KV_DIM)), b3((nblk, N_META, KV_DIM)), b3((nblk, N_META, KV_DIM)),
                  b3((nblk, w, KV_DIM)), b3((nblk, w, KV_DIM)), b3((nblk, 1, KV_DIM)), b3((nblk, 1, KV_DIM)),
                  pl.BlockSpec((1, N_HEADS, 1), lambda i: (0, 0, 0))],
        out_specs=b3((nblk, N_HEADS, KV_DIM)),
        out_shape=jax.ShapeDtypeStruct((n, N_HEADS, KV_DIM), F32), name="sample_attn",
        compiler_params=pltpu.CompilerParams(dimension_semantics=("arbitrary",), vmem_limit_bytes=VMEM_LIMIT),
    )(qblk, mk, mv, wk, wv, nk, nv, sink3)


def _sample_ssd_kernel(xdt_ref, dec_ref, b_ref, c_ref, st_ref, yattn_ref, xs_ref, z_ref, ga_ref, gs_ref, x_ref,
                       dskip_ref, nw_ref, wo_ref, g1_ref, b1_ref, wrh_ref, wrl_ref, br_ref,
                       st_out_ref, h1_ref, rl_ref, xdt_t_sc, dec_t_sc, y_t_sc, *, nblk):
    i = pl.program_id(0)
    n_all = xdt_ref.shape[0]

    @pl.when(i == 0)
    def _():
        xdt_t_sc[...] = xdt_ref[...].T
        dec_t_sc[...] = dec_ref[...].T
        y_t_sc[...] = jnp.zeros_like(y_t_sc)

    r_id = lax.broadcasted_iota(jnp.int32, (n_all, LANES), 0)
    c_id = lax.broadcasted_iota(jnp.int32, (n_all, LANES), 1)
    for t in range(nblk):
        n = i * nblk + t
        pick_row = (r_id == n).astype(BF16)
        pick_col = (c_id == n).astype(BF16)
        xb_n = _dot_exact_rhs(xdt_t_sc[...], pick_row)
        db_n = _dot_exact_rhs(dec_t_sc[...], pick_row)
        brow = b_ref[pl.ds(n, 1), :]
        crow = c_ref[pl.ds(n, 1), :]
        gw = SSM_INNER // SSM_GROUPS
        bfull = jnp.concatenate([jnp.broadcast_to(brow[:, g * SSM_STATE:(g + 1) * SSM_STATE], (gw, SSM_STATE))
                                 for g in range(SSM_GROUPS)], axis=0)
        cfull = jnp.concatenate([jnp.broadcast_to(crow[:, g * SSM_STATE:(g + 1) * SSM_STATE], (gw, SSM_STATE))
                                 for g in range(SSM_GROUPS)], axis=0)
        h_new = st_ref[t] * db_n + xb_n * bfull
        st_out_ref[t] = h_new
        y_t_sc[...] += _dot_exact_rhs(h_new * cfull, pick_col)

    @pl.when(i == pl.num_programs(0) - 1)
    def _():
        xs = xs_ref[...]
        y = y_t_sc[...].T + dskip_ref[...] * xs
        y_ssd = _gated_rmsnorm(y, z_ref[...], nw_ref[...])
        mixin = jax.nn.sigmoid(ga_ref[...]) * yattn_ref[...] + jax.nn.sigmoid(gs_ref[...]) * y_ssd
        mix = _dot(mixin.astype(BF16), wo_ref[...])
        h1 = _layer_norm(ALPHA * x_ref[...] + mix, g1_ref[...], b1_ref[...])
        h1_ref[...] = jnp.zeros_like(h1_ref)
        rl_ref[...] = jnp.zeros_like(rl_ref)
        _to_tiles(h1_ref, h1)
        rl_ref[0:n_all, :] = _router_logits(h1, wrh_ref[...], wrl_ref[...], br_ref[...])


def _sample_ssd_call(xdt, dec, bmat, cmat, state3, y_attn, xs, z, ga, gs, x2d, dskip_e, nw, w_o_b, g1, b1,
                     wr_hi, wr_lo, br, nblk):
    n = xdt.shape[0]
    full = lambda a: pl.BlockSpec(a.shape, lambda i: (0,) * a.ndim)
    st_spec = pl.BlockSpec((nblk, SSM_INNER, SSM_STATE), lambda i: (i, 0, 0))
    ins = [xdt, dec, bmat, cmat, state3, y_attn, xs, z, ga, gs, x2d, dskip_e, nw, w_o_b, g1, b1, wr_hi, wr_lo, br]
    in_specs = [full(a) for a in ins]
    in_specs[4] = st_spec
    return pl.pallas_call(
        functools.partial(_sample_ssd_kernel, nblk=nblk),
        grid=(n // nblk,),
        in_specs=in_specs,
        out_specs=[st_spec,
                   pl.BlockSpec((TOKEN_BLOCK * D_TILES, LANES), lambda i: (0, 0)),
                   pl.BlockSpec((TOKEN_BLOCK, LANES), lambda i: (0, 0))],
        out_shape=[jax.ShapeDtypeStruct(state3.shape, F32),
                   jax.ShapeDtypeStruct((TOKEN_BLOCK * D_TILES, LANES), F32),
                   jax.ShapeDtypeStruct((TOKEN_BLOCK, LANES), F32)],
        scratch_shapes=[pltpu.VMEM((SSM_INNER, n), F32), pltpu.VMEM((SSM_INNER, n), F32),
                        pltpu.VMEM((SSM_INNER, n), F32)],
        name="sample_ssd",
        compiler_params=pltpu.CompilerParams(dimension_semantics=("arbitrary",), vmem_limit_bytes=VMEM_LIMIT),
    )(*ins)


def _top2_experts(rl):
    lane = lax.broadcasted_iota(jnp.int32, rl.shape, 1)
    big = jnp.int32(1 << 20)
    is_g = lane < N_EXPERT_GROUPS
    gmax = jnp.max(jnp.where(is_g, rl, -jnp.inf), axis=1, keepdims=True)
    gsel = jnp.min(jnp.where(is_g & (rl == gmax), lane, big), axis=1, keepdims=True)
    gden = jnp.sum(jnp.where(is_g, jnp.exp(rl - gmax), 0.0), axis=1, keepdims=True)
    g_w = 1.0 / gden
    e_id = lane - ROUTE_OFF
    in_grp = (e_id >= 0) & (e_id < N_EXPERTS) & ((e_id // EXPERTS_PER_GROUP) == gsel)
    v1 = jnp.max(jnp.where(in_grp, rl, -jnp.inf), axis=1, keepdims=True)
    i1 = jnp.min(jnp.where(in_grp & (rl == v1), lane, big), axis=1, keepdims=True)
    rest = in_grp & (lane != i1)
    v2 = jnp.max(jnp.where(rest, rl, -jnp.inf), axis=1, keepdims=True)
    i2 = jnp.min(jnp.where(rest & (rl == v2), lane, big), axis=1, keepdims=True)
    e21 = jnp.exp(v2 - v1)
    p1 = g_w / (1.0 + e21)
    p2 = g_w * e21 / (1.0 + e21)
    return i1 - ROUTE_OFF, i2 - ROUTE_OFF, p1, p2


R_E1, R_E2, R_P1, R_P2, R_RANK1, R_RANK2 = range(6)


def _route_kernel(rlp_ref, rls_ref, route_ref, counts_ref, tri_sc, run_sc, *, n_prompt_blocks):
    i = pl.program_id(0)
    tb = rlp_ref.shape[0]

    @pl.when(i == 0)
    def _():
        r = lax.broadcasted_iota(jnp.int32, (tb, tb), 0)
        c = lax.broadcasted_iota(jnp.int32, (tb, tb), 1)
        tri_sc[...] = (r > c).astype(BF16)
        run_sc[...] = jnp.zeros_like(run_sc)

    e1, e2, p1, p2 = _top2_experts(jnp.where(i < n_prompt_blocks, rlp_ref[...], rls_ref[...]))
    lane = lax.broadcasted_iota(jnp.int32, (tb, LANES), 1)
    oh1 = (lane == e1).astype(F32)
    oh2 = (lane == e2).astype(F32)
    both = oh1 + oh2
    before = _dot(tri_sc[...], both.astype(BF16)) + run_sc[...]
    rank1 = jnp.sum(before * oh1, axis=1, keepdims=True)
    rank2 = jnp.sum(before * oh2, axis=1, keepdims=True)
    run_sc[...] += jnp.sum(both, axis=0, keepdims=True)
    rec = jnp.zeros((tb, LANES), F32)
    for k, val in ((R_E1, e1.astype(F32)), (R_E2, e2.astype(F32)), (R_P1, p1), (R_P2, p2),
                   (R_RANK1, rank1), (R_RANK2, rank2)):
        rec = jnp.where(lane == k, val, rec)
    route_ref[...] = rec
    counts_ref[...] = run_sc[...]


def _route_call(rl_p, rl_s):
    tb = TOKEN_BLOCK
    npb = rl_p.shape[0] // tb
    t = rl_p.shape[0] + rl_s.shape[0]
    return pl.pallas_call(
        functools.partial(_route_kernel, n_prompt_blocks=npb),
        grid=(t // tb,),
        in_specs=[pl.BlockSpec((tb, LANES), lambda i: (jnp.minimum(i, npb - 1), 0)),
                  pl.BlockSpec((tb, LANES), lambda i: (0, 0))],
        out_specs=[pl.BlockSpec((tb, LANES), lambda i: (i, 0)), pl.BlockSpec((1, LANES), lambda i: (0, 0))],
        out_shape=[jax.ShapeDtypeStruct((t, LANES), F32), jax.ShapeDtypeStruct((1, LANES), F32)],
        scratch_shapes=[pltpu.VMEM((tb, tb), BF16), pltpu.VMEM((1, LANES), F32)],
        name="moe_route",
        compiler_params=pltpu.CompilerParams(dimension_semantics=("arbitrary",), vmem_limit_bytes=VMEM_LIMIT),
    )(rl_p, rl_s)


def _expert_layout(counts_row, n_tiles):
    tr = EXPERT_ROWS
    counts = counts_row[0, :N_EXPERTS].astype(jnp.int32)
    padded = ((counts + tr - 1) // tr) * tr
    ends = jnp.cumsum(padded)
    starts = ends - padded
    n_used = ends[-1] // tr
    tile_id = jnp.arange(n_tiles, dtype=jnp.int32)
    tile_expert = jnp.sum((jnp.minimum(tile_id, n_used - 1)[:, None] * tr >= ends[None, :]).astype(jnp.int32), axis=1)
    is_last = jnp.any((counts > 0)[None, :] & (tile_id[:, None] == (ends // tr - 1)[None, :]), axis=1)
    zero_tile = (is_last | (tile_id >= n_used)).astype(jnp.int32)
    starts_row = jnp.pad(starts.astype(F32), (0, LANES - N_EXPERTS))[None, :]
    return starts_row, tile_expert, n_used.reshape(1), zero_tile


def _positions_kernel(route_ref, starts_ref, pos_ref):
    rec = route_ref[...]
    lane = lax.broadcasted_iota(jnp.int32, rec.shape, 1)
    starts = starts_ref[...]
    start1 = jnp.sum(jnp.where(lane == rec[:, R_E1:R_E1 + 1].astype(jnp.int32), starts, 0.0), axis=1, keepdims=True)
    start2 = jnp.sum(jnp.where(lane == rec[:, R_E2:R_E2 + 1].astype(jnp.int32), starts, 0.0), axis=1, keepdims=True)
    pos = jnp.where(lane == 0, start1 + rec[:, R_RANK1:R_RANK1 + 1],
                    jnp.where(lane == 1, start2 + rec[:, R_RANK2:R_RANK2 + 1], 0.0))
    pos_ref[...] = pos.T[0:SUBLANES, :].astype(jnp.int32)


def _positions_call(route, starts_row):
    t = route.shape[0]
    tb = TOKEN_BLOCK
    return pl.pallas_call(
        _positions_kernel,
        grid=(t // tb,),
        in_specs=[pl.BlockSpec((tb, LANES), lambda i: (i, 0)), pl.BlockSpec((1, LANES), lambda i: (0, 0))],
        out_specs=pl.BlockSpec((SUBLANES, tb), lambda i: (0, i)),
        out_shape=jax.ShapeDtypeStruct((SUBLANES, t), jnp.int32),
        name="moe_positions",
        compiler_params=pltpu.CompilerParams(dimension_semantics=("arbitrary",)),
    )(route, starts_row)


def _row_copy(src, dst, sem):
    return pltpu.make_async_copy(src, dst, sem)


def _dispatch_kernel(pos1_ref, pos2_ref, zt_ref, hp_ref, hs_ref, xs_hbm, zero_sc, sem, *, n_prompt_blocks):
    i = pl.program_id(0)
    tb = TOKEN_BLOCK
    tr = EXPERT_ROWS

    @pl.when(i == 0)
    def _():
        zero_sc[...] = jnp.zeros_like(zero_sc)

        def ztile(m):
            rows = tr * D_TILES
            return _row_copy(zero_sc, xs_hbm.at[pl.ds(pl.multiple_of(m * rows, rows), rows)], sem)

        def zstart(m, c):
            @pl.when(zt_ref[m] != 0)
            def _():
                ztile(m).start()
            return c

        def zwait(m, c):
            @pl.when(zt_ref[m] != 0)
            def _():
                ztile(m).wait()
            return c

        lax.fori_loop(0, zt_ref.shape[0], zstart, 0)
        lax.fori_loop(0, zt_ref.shape[0], zwait, 0)

    def scatter(src_ref):
        def copies(r):
            t = i * tb + r
            return (_row_copy(src_ref.at[_token_rows(r)], xs_hbm.at[_token_rows(pos1_ref[t])], sem),
                    _row_copy(src_ref.at[_token_rows(r)], xs_hbm.at[_token_rows(pos2_ref[t])], sem))

        def start(r, c):
            a, b = copies(r)
            a.start(priority=0)
            b.start(priority=1)
            return c

        def wait(r, c):
            a, b = copies(r)
            a.wait()
            b.wait()
            return c

        lax.fori_loop(0, tb, start, 0, unroll=8)
        lax.fori_loop(0, tb, wait, 0, unroll=8)

    @pl.when(i < n_prompt_blocks)
    def _():
        scatter(hp_ref)

    @pl.when(i >= n_prompt_blocks)
    def _():
        scatter(hs_ref)


def _dispatch_call(pos1, pos2, zero_tile, h1_p, h1_s, n_tiles):
    tb = TOKEN_BLOCK
    npb = h1_p.shape[0] // (tb * D_TILES)
    t = (h1_p.shape[0] + h1_s.shape[0]) // D_TILES
    grid_spec = pltpu.PrefetchScalarGridSpec(
        num_scalar_prefetch=3,
        grid=(t // tb,),
        in_specs=[pl.BlockSpec((tb * D_TILES, LANES), lambda i, *_: (jnp.minimum(i, npb - 1), 0)),
                  pl.BlockSpec((tb * D_TILES, LANES), lambda i, *_: (0, 0))],
        out_specs=pl.BlockSpec(memory_space=pl.ANY),
        scratch_shapes=[pltpu.VMEM((EXPERT_ROWS * D_TILES, LANES), F32), pltpu.SemaphoreType.DMA(())],
    )
    return pl.pallas_call(
        functools.partial(_dispatch_kernel, n_prompt_blocks=npb), grid_spec=grid_spec,
        out_shape=jax.ShapeDtypeStruct((n_tiles * EXPERT_ROWS * D_TILES, LANES), F32), name="moe_dispatch",
        compiler_params=pltpu.CompilerParams(dimension_semantics=("arbitrary",), vmem_limit_bytes=VMEM_LIMIT),
    )(pos1, pos2, zero_tile, h1_p, h1_s)


def _expert_kernel(te_ref, nu_ref, x_ref, wg_ref, wu_ref, wd_ref, y_ref, wg_sc, wu_sc, wd_sc):
    i = pl.program_id(0)
    used = i < nu_ref[0]

    @pl.when((i == 0) | (te_ref[i] != te_ref[jnp.maximum(i - 1, 0)]))
    def _():
        wg_sc[...] = wg_ref[...].astype(BF16)
        wu_sc[...] = wu_ref[...].astype(BF16)
        wd_sc[...] = wd_ref[...].astype(BF16)

    @pl.when(used)
    def _():
        x = _from_tiles(x_ref, EXPERT_ROWS).astype(BF16)
        hid = _silu(_dot(x, wg_sc[...])) * _dot(x, wu_sc[...])
        _to_tiles(y_ref, _dot(hid.astype(BF16), wd_sc[...]))

    @pl.when(jnp.logical_not(used))
    def _():
        y_ref[...] = jnp.zeros_like(y_ref)


def _expert_call(tile_expert, n_used, xs, wg, wu, wd, n_tiles):
    tr = EXPERT_ROWS
    rows = lambda i, te, nu: (jnp.minimum(i, nu[0] - 1), 0)
    out_rows = lambda i, te, nu: (i, 0)
    wsel = lambda i, te, nu: (te[i], 0, 0)
    grid_spec = pltpu.PrefetchScalarGridSpec(
        num_scalar_prefetch=2,
        grid=(n_tiles,),
        in_specs=[pl.BlockSpec((tr * D_TILES, LANES), rows),
                  pl.BlockSpec((None, D_MODEL, D_EXPERT), wsel),
                  pl.BlockSpec((None, D_MODEL, D_EXPERT), wsel),
                  pl.BlockSpec((None, D_EXPERT, D_MODEL), wsel)],
        out_specs=pl.BlockSpec((tr * D_TILES, LANES), out_rows),
        scratch_shapes=[pltpu.VMEM((D_MODEL, D_EXPERT), BF16), pltpu.VMEM((D_MODEL, D_EXPERT), BF16),
                        pltpu.VMEM((D_EXPERT, D_MODEL), BF16)],
    )
    return pl.pallas_call(
        _expert_kernel, grid_spec=grid_spec,
        out_shape=jax.ShapeDtypeStruct((n_tiles * tr * D_TILES, LANES), F32), name="moe_experts",
        compiler_params=pltpu.CompilerParams(dimension_semantics=("arbitrary",), vmem_limit_bytes=VMEM_LIMIT),
    )(tile_expert, n_used, xs, wg, wu, wd)


def _combine_kernel(pos1_ref, pos2_ref, hp_ref, hs_ref, route_ref, g2_ref, b2_ref, ys_hbm, op_ref, os_ref, ybuf, sem,
                    *, n_prompt_blocks):
    i = pl.program_id(0)
    n = pl.num_programs(0)
    tb = TOKEN_BLOCK

    def copies(blk, slot, r):
        t = blk * tb + r
        return (_row_copy(ys_hbm.at[_token_rows(pos1_ref[t])], ybuf.at[slot, 0, _token_rows(r)], sem.at[slot]),
                _row_copy(ys_hbm.at[_token_rows(pos2_ref[t])], ybuf.at[slot, 1, _token_rows(r)], sem.at[slot]))

    def issue(blk, slot):
        def body(r, c):
            a, b = copies(blk, slot, r)
            a.start(priority=0)
            b.start(priority=1)
            return c
        lax.fori_loop(0, tb, body, 0, unroll=8)

    def drain(blk, slot):
        def body(r, c):
            a, b = copies(blk, slot, r)
            a.wait()
            b.wait()
            return c
        lax.fori_loop(0, tb, body, 0, unroll=8)

    @pl.when(i == 0)
    def _():
        issue(0, 0)

    @pl.when(i + 1 < n)
    def _():
        issue(i + 1, (i + 1) % 2)

    slot = i % 2
    drain(i, slot)
    route = route_ref[...]
    p1 = route[:, R_P1:R_P1 + 1]
    p2 = route[:, R_P2:R_P2 + 1]
    h = jnp.where(i < n_prompt_blocks, _from_tiles(hp_ref, tb), _from_tiles(hs_ref, tb))
    moe = p1 * _from_tiles(ybuf.at[slot, 0], tb) + p2 * _from_tiles(ybuf.at[slot, 1], tb)
    out = _layer_norm(ALPHA * h + moe, g2_ref[...], b2_ref[...])

    @pl.when(i < n_prompt_blocks)
    def _():
        op_ref[...] = out

    @pl.when(i >= n_prompt_blocks)
    def _():
        os_ref[...] = out


def _combine_call(pos1, pos2, h1_p, h1_s, route, g2, b2, ys):
    n_prompt_tokens = h1_p.shape[0] // D_TILES
    t = n_prompt_tokens + h1_s.shape[0] // D_TILES
    tb = TOKEN_BLOCK
    npb = n_prompt_tokens // tb
    grid_spec = pltpu.PrefetchScalarGridSpec(
        num_scalar_prefetch=2,
        grid=(t // tb,),
        in_specs=[pl.BlockSpec((tb * D_TILES, LANES), lambda i, *_: (jnp.minimum(i, npb - 1), 0)),
                  pl.BlockSpec((tb * D_TILES, LANES), lambda i, *_: (0, 0)),
                  pl.BlockSpec((tb, LANES), lambda i, *_: (i, 0)),
                  pl.BlockSpec((1, D_MODEL), lambda i, *_: (0, 0)),
                  pl.BlockSpec((1, D_MODEL), lambda i, *_: (0, 0)),
                  pl.BlockSpec(memory_space=pl.ANY)],
        out_specs=[pl.BlockSpec((tb, D_MODEL), lambda i, *_: (jnp.minimum(i, npb - 1), 0)),
                   pl.BlockSpec((tb, D_MODEL), lambda i, *_: (0, 0))],
        scratch_shapes=[pltpu.VMEM((2, TOP_K, tb * D_TILES, LANES), F32), pltpu.SemaphoreType.DMA((2,))],
    )
    return pl.pallas_call(
        functools.partial(_combine_kernel, n_prompt_blocks=npb), grid_spec=grid_spec,
        out_shape=[jax.ShapeDtypeStruct((n_prompt_tokens, D_MODEL), F32),
                   jax.ShapeDtypeStruct((tb, D_MODEL), F32)],
        name="moe_combine",
        compiler_params=pltpu.CompilerParams(dimension_semantics=("arbitrary",), vmem_limit_bytes=VMEM_LIMIT),
    )(pos1, pos2, h1_p, h1_s, route, g2, b2, ys)


def _rope_tables(pos):
    inv_freq = jnp.exp(-math.log(ROPE_THETA) * jnp.arange(HALF, dtype=F32) * (2.0 / HEAD_DIM))
    ang = pos.astype(F32)[:, None] * inv_freq[None, :]
    cos = jnp.cos(ang)
    sin = jnp.sin(ang)
    cos_t = jnp.tile(jnp.concatenate([cos, cos], axis=1), (1, LANES // HEAD_DIM))
    sin_t = jnp.tile(jnp.concatenate([-sin, sin], axis=1), (1, LANES // HEAD_DIM))
    return cos_t, sin_t


def _pad_lanes(a, n=LANES):
    return jnp.pad(a, ((0, 0), (0, n - a.shape[1])))


def kernel(x_prompt, x_sample, cache_meta_k, cache_meta_v, cache_win_k, cache_win_v, state_conv, state_ssm,
           meta_tokens, w_in, attn_sinks, conv_w, conv_b, dt_bias, a_log, d_skip, ssm_norm_w, w_o,
           ln1_g, ln1_b, router_group_w, router_group_b, router_expert_w, router_expert_b,
           w_gate, w_up, w_down, ln2_g, ln2_b):
    B, S, D = x_prompt.shape
    N = x_sample.shape[0]
    l = 0

    wi = w_in[l].astype(BF16)
    dt0 = C_GA
    w_in_r = jnp.concatenate([wi[:, :dt0], wi[:, dt0 + SSM_HEADS:], wi[:, dt0:dt0 + SSM_HEADS],
                              jnp.zeros((D, LANES - SSM_HEADS), wi.dtype)], axis=1)
    w_o_b = w_o[l].astype(BF16)
    wr = _pad_lanes(jnp.concatenate([router_group_w[l], router_expert_w[l]], axis=1))
    wr_hi = wr.astype(BF16)
    wr_lo = (wr - wr_hi.astype(F32)).astype(BF16)
    br = _pad_lanes(jnp.concatenate([router_group_b[l], router_expert_b[l]])[None, :])
    dtb = _pad_lanes(dt_bias[l][None, :])
    alog = _pad_lanes(a_log[l][None, :])
    dskip_e = jnp.repeat(d_skip[l], SSM_HEAD_DIM)[None, :]
    nw = ssm_norm_w[l][None, :]
    g1, b1 = ln1_g[l][None, :], ln1_b[l][None, :]
    g2, b2 = ln2_g[l][None, :], ln2_b[l][None, :]
    cw_, cb_ = conv_w[l], conv_b[l][None, :]
    sinks = attn_sinks[l]
    wg_b, wu_b, wd_b = w_gate[l], w_up[l], w_down[l]

    nb = S // BLOCK + 1
    pos_p = jnp.arange(nb * BLOCK, dtype=jnp.int32) - META_PAD
    cos_p, sin_p = _rope_tables(pos_p)
    meta_blk = jnp.concatenate([jnp.zeros((META_PAD, D), F32), meta_tokens.astype(F32)], axis=0)
    (h1_p, rl_p, p_mk, p_mv, p_wk, p_wv, p_conv, p_ssm) = _prompt_call(
        x_prompt, meta_blk, w_in_r, w_o_b, cos_p, sin_p, cw_, cb_, dtb, alog, dskip_e, nw, g1, b1,
        wr_hi, wr_lo, br, sinks)

    xs2d = x_sample.reshape(N, D)
    cos_s, sin_s = _rope_tables(jnp.full((1,), PAST_LEN, jnp.int32))
    sc = state_conv[l]
    expand = (jnp.arange(LANES, dtype=jnp.int32)[:, None]
              == (jnp.arange(SSM_INNER, dtype=jnp.int32) // SSM_HEAD_DIM)[None, :]).astype(BF16)
    (q_s, k_s, v_s, z_s, xraw_s, xdt_s, dec_s, xsc_s, b_s, c_s, ga_s, gs_s) = _sample_proj_call(
        xs2d, w_in_r, cos_s, sin_s, sc[:, 0], sc[:, 1], sc[:, 2], cw_, cb_, dtb, alog, expand)
    head_group = jnp.arange(N_HEADS, dtype=jnp.int32) // Q_PER_KV
    grp_mask = (head_group[:, None] == jnp.arange(N_KV_HEADS, dtype=jnp.int32)[None, :]).astype(F32)
    qblk = (q_s.reshape(N, N_HEADS, 1, HEAD_DIM) * grp_mask[None, :, :, None]).reshape(N, N_HEADS, KV_DIM)
    wlen = cache_win_k.shape[2]
    o_blk = _sample_attn_call(
        qblk, cache_meta_k[l].reshape(N, N_META, KV_DIM), cache_meta_v[l].reshape(N, N_META, KV_DIM),
        cache_win_k[l].reshape(N, wlen, KV_DIM), cache_win_v[l].reshape(N, wlen, KV_DIM),
        k_s.reshape(N, 1, KV_DIM), v_s.reshape(N, 1, KV_DIM), sinks.reshape(1, N_HEADS, 1), 16)
    o4 = o_blk.reshape(N, N_HEADS, N_KV_HEADS, HEAD_DIM)
    y_attn_s = jnp.sum(o4 * grp_mask[None, :, :, None], axis=2).reshape(N, Q_DIM)
    s_ssm3, h1_s, rl_s = _sample_ssd_call(
        xdt_s, dec_s, b_s, c_s, state_ssm[l].reshape(N, SSM_INNER, SSM_STATE), y_attn_s, xsc_s, z_s, ga_s, gs_s,
        xs2d, dskip_e, nw, w_o_b, g1, b1, wr_hi, wr_lo, br, 8)

    n_tok = rl_p.shape[0] + rl_s.shape[0]
    n_tiles = (TOP_K * n_tok) // EXPERT_ROWS + N_EXPERTS
    route, counts = _route_call(rl_p, rl_s)
    starts_row, tile_expert, n_used, zero_tile = _expert_layout(counts, n_tiles)
    pos = _positions_call(route, starts_row)
    pos1, pos2 = pos[0], pos[1]
    xs_sorted = _dispatch_call(pos1, pos2, zero_tile, h1_p, h1_s, n_tiles)
    ys_sorted = _expert_call(tile_expert, n_used, xs_sorted, wg_b, wu_b, wd_b, n_tiles)
    y_p2d, y_s2d = _combine_call(pos1, pos2, h1_p, h1_s, route, g2, b2, ys_sorted)
    y_prompt = y_p2d.reshape(B, S, D)
    y_sample = y_s2d[:N].reshape(N, 1, D)

    kv5 = lambda a, n, t: a.reshape(1, n, t, N_KV_HEADS, HEAD_DIM)
    s_conv = jnp.stack([sc[:, 1], sc[:, 2], xraw_s], axis=1)[None]
    return (y_prompt, y_sample,
            kv5(p_mk, B, N_META), kv5(p_mv, B, N_META), kv5(p_wk, B, WINDOW), kv5(p_wv, B, WINDOW),
            p_conv[None], p_ssm.reshape(1, B, SSM_HEADS, SSM_HEAD_DIM, SSM_STATE),
            kv5(k_s, N, 1), kv5(v_s, N, 1), s_conv,
            s_ssm3.reshape(1, N, SSM_HEADS, SSM_HEAD_DIM, SSM_STATE))
```

```python
import functools
import math

import jax
import jax.numpy as jnp
from jax import lax
from jax.experimental import pallas as pl
from jax.experimental.pallas import tpu as pltpu

F32 = jnp.float32
BF16 = jnp.bfloat16

D_MODEL = 1024
N_META = 16
N_HEADS = 16
N_KV_HEADS = 4
HEAD_DIM = 64
HALF = HEAD_DIM // 2
Q_PER_KV = N_HEADS // N_KV_HEADS
WINDOW = 128
BLOCK = 128
META_PAD = BLOCK - N_META
ROPE_THETA = 10000.0
PAST_LEN = 16384
SSM_INNER = D_MODEL
SSM_HEAD_DIM = 64
SSM_HEADS = SSM_INNER // SSM_HEAD_DIM
SSM_GROUPS = 4
SSM_STATE = 128
CONV_WIDTH = 4
GN = SSM_GROUPS * SSM_STATE
CONV_DIM = SSM_INNER + 2 * GN
Q_DIM = N_HEADS * HEAD_DIM
KV_DIM = N_KV_HEADS * HEAD_DIM
N_EXPERT_GROUPS = 4
EXPERTS_PER_GROUP = 8
N_EXPERTS = N_EXPERT_GROUPS * EXPERTS_PER_GROUP
D_EXPERT = 256
DEPTH = 1
ALPHA = (2 * DEPTH) ** 0.25
LN_EPS = 1e-5
RMS_EPS = 1e-5
NEG_INF = -1e30

LANES = 128
SUBLANES = 8
VMEM_LIMIT = 56 * 1024 * 1024

C_Q = 0
C_K = C_Q + Q_DIM
C_V = C_K + KV_DIM
C_Z = C_V + KV_DIM
C_XBC = C_Z + SSM_INNER
C_GA = C_XBC + CONV_DIM
C_GS = C_GA + D_MODEL
C_DT = C_GS + D_MODEL
IN_PAD = C_DT + LANES
ROUTE_OFF = N_EXPERT_GROUPS
D_TILES = D_MODEL // LANES
TOKEN_BLOCK = 512
EXPERT_ROWS = 512
TOP_K = 2


def _to_tiles(ref, x):
    n = x.shape[0]
    for c in range(D_TILES):
        ref[pl.ds(c, n, stride=D_TILES), :] = x[:, c * LANES:(c + 1) * LANES]


def _from_tiles(ref, n):
    return jnp.concatenate([ref[pl.ds(c, n, stride=D_TILES), :] for c in range(D_TILES)], axis=1)


def _token_rows(t):
    return pl.ds(pl.multiple_of(t * D_TILES, D_TILES), D_TILES)


def _const_spec(shape):
    nd = len(shape)
    return pl.BlockSpec(shape, lambda *_: (0,) * nd, pipeline_mode=pl.Buffered(1))


def _dot(a, b):
    return jnp.dot(a, b, preferred_element_type=F32)


def _dot_nt(a, b):
    return lax.dot_general(a, b, (((1,), (1,)), ((), ())), preferred_element_type=F32)


def _dot_tn(a, b):
    return lax.dot_general(a, b, (((0,), (0,)), ((), ())), preferred_element_type=F32)


def _split3(x):
    p1 = x.astype(BF16)
    r1 = x - p1.astype(F32)
    p2 = r1.astype(BF16)
    r2 = r1 - p2.astype(F32)
    return p1, p2, r2.astype(BF16)


def _dot_exact_rhs(x, m_bf16):
    p1, p2, p3 = _split3(x)
    return _dot(p1, m_bf16) + _dot(p2, m_bf16) + _dot(p3, m_bf16)


def _dot_exact_lhs(m_bf16, x):
    p1, p2, p3 = _split3(x)
    return _dot(m_bf16, p1) + _dot(m_bf16, p2) + _dot(m_bf16, p3)


def _silu(x):
    return x * jax.nn.sigmoid(x)


def _softplus(x):
    return jnp.maximum(x, 0.0) + jnp.log1p(jnp.exp(-jnp.abs(x)))


def _layer_norm(x, g, b):
    mu = jnp.mean(x, -1, keepdims=True)
    xc = x - mu
    var = jnp.mean(xc * xc, -1, keepdims=True)
    return xc * lax.rsqrt(var + LN_EPS) * g + b


def _rope_tiles(t, cos, sin_signed):
    lane = lax.broadcasted_iota(jnp.int32, (t.shape[0], LANES), 1)
    first_half = (lane % HEAD_DIM) < HALF
    outs = []
    for i in range(t.shape[1] // LANES):
        x = t[:, i * LANES:(i + 1) * LANES]
        partner = jnp.where(first_half, pltpu.roll(x, LANES - HALF, 1), pltpu.roll(x, HALF, 1))
        outs.append(x * cos + partner * sin_signed)
    return jnp.concatenate(outs, axis=1) if len(outs) > 1 else outs[0]


def _gated_rmsnorm(y, z, nw):
    yf = y * _silu(z)
    gw = SSM_INNER // SSM_GROUPS
    outs = []
    for g in range(SSM_GROUPS):
        t = yf[:, g * gw:(g + 1) * gw]
        ms = jnp.mean(t * t, -1, keepdims=True)
        outs.append(t * lax.rsqrt(ms + RMS_EPS))
    return jnp.concatenate(outs, axis=1) * nw


def _router_logits(h, wr_hi, wr_lo, br):
    h_hi = h.astype(BF16)
    h_lo = (h - h_hi.astype(F32)).astype(BF16)
    return _dot(h_hi, wr_hi) + _dot(h_lo, wr_hi) + _dot(h_hi, wr_lo) + br


PHASES = 2


def _prompt_kernel(sinks_ref, xa_ref, xb_ref, meta_ref, bias_ref, win_ref, wo_ref, cos_ref, sin_ref, convw_ref,
                   convb_ref, dtb_ref, alog_ref, dskip_ref, nw_ref, g1_ref, b1_ref, wrh_ref, wrl_ref, br_ref,
                   h1_ref, rl_ref, kmeta_ref, vmeta_ref, kwin_ref, vwin_ref, conv_ref, ssm_ref,
                   kd_sc, vd_sc, xbuf_sc, q_st, kd_st, vd_st, z_st, xraw_st, dt_st, ga_st, gs_st, x_st):
    step = pl.program_id(1)
    staging = (q_st, kd_st, vd_st, z_st, xraw_st, dt_st, ga_st, gs_st, x_st)

    @pl.when(step == 0)
    def _():
        kd_sc[...] = jnp.zeros_like(kd_sc)
        vd_sc[...] = jnp.zeros_like(vd_sc)
        xbuf_sc[0:SUBLANES, :] = jnp.zeros((SUBLANES, CONV_DIM), F32)
        ssm_ref[...] = jnp.zeros_like(ssm_ref)
        kmeta_ref[...] = jnp.zeros_like(kmeta_ref)
        vmeta_ref[...] = jnp.zeros_like(vmeta_ref)
        for ref in staging:
            ref[...] = jnp.zeros_like(ref)

    for ph, x_ref in enumerate((xa_ref, xb_ref)):
        _prompt_phase(step * PHASES + ph, ph, sinks_ref, x_ref, meta_ref, bias_ref, win_ref, wo_ref, cos_ref, sin_ref,
                      convw_ref, convb_ref, dtb_ref, alog_ref, dskip_ref, nw_ref, g1_ref, b1_ref, wrh_ref, wrl_ref,
                      br_ref, h1_ref, rl_ref, kmeta_ref, vmeta_ref, kwin_ref, vwin_ref, conv_ref, ssm_ref,
                      kd_sc, vd_sc, xbuf_sc, *staging)


def _prompt_phase(j, ph, sinks_ref, x_ref, meta_ref, bias_ref, win_ref, wo_ref, cos_ref, sin_ref, convw_ref, convb_ref,
                  dtb_ref, alog_ref, dskip_ref, nw_ref, g1_ref, b1_ref, wrh_ref, wrl_ref, br_ref,
                  h1_ref, rl_ref, kmeta_ref, vmeta_ref, kwin_ref, vwin_ref, conv_ref, ssm_ref,
                  kd_sc, vd_sc, xbuf_sc, q_st, kd_st, vd_st, z_st, xraw_st, dt_st, ga_st, gs_st, x_st):
    jc = j - 1
    sp, sc = 1, 0
    T = BLOCK
    st_sc = ssm_ref
    staging = (q_st, kd_st, vd_st, z_st, xraw_st, dt_st, ga_st, gs_st, x_st)

    lane = lax.broadcasted_iota(jnp.int32, (T, LANES), 1)
    row = lax.broadcasted_iota(jnp.int32, (T, LANES), 0)
    lo = lane < HEAD_DIM

    x_new = jnp.where(j == 0, meta_ref[...], x_ref[...])
    xb = x_new.astype(BF16)
    n_blocks = cos_ref.shape[0] // T
    tab_rows = pl.ds(pl.multiple_of(jnp.minimum(j, n_blocks - 1) * T, T), T)
    cos = cos_ref[tab_rows, :]
    sin = sin_ref[tab_rows, :]
    def dup(t, g):
        tile = t[:, (g // 2) * LANES:(g // 2 + 1) * LANES]
        rolled = pltpu.roll(tile, HEAD_DIM, 1)
        return jnp.where(lo if g % 2 == 0 else jnp.logical_not(lo), tile, rolled)

    W2 = 2 * LANES

    def proj(col, width=W2):
        return _dot(xb, win_ref[:, col:col + width])

    def task_q(t):
        def run():
            q = _rope_tiles(proj(C_Q + t * W2), cos, sin)
            q_st[sp, :, t * W2:(t + 1) * W2] = (q * (HEAD_DIM ** -0.5)).astype(BF16)
        return run

    def task_k():
        k = _rope_tiles(proj(C_K), cos, sin)
        kwin_ref[...] = k
        kmeta_ref[...] = jnp.where(j == 0, k[META_PAD:, :], kmeta_ref[...])
        kd_st[sp] = jnp.concatenate([dup(k, g) for g in range(N_KV_HEADS)], axis=1).astype(BF16)

    def task_v():
        v = proj(C_V)
        vwin_ref[...] = v
        vmeta_ref[...] = jnp.where(j == 0, v[META_PAD:, :], vmeta_ref[...])
        vd_st[sp] = jnp.concatenate([dup(v, g) for g in range(N_KV_HEADS)], axis=1).astype(BF16)

    def task_plain(ref, col0, t):
        def run():
            ref[sp, :, t * W2:(t + 1) * W2] = proj(col0 + t * W2)
        return run

    def task_tail():
        dt_st[sp] = proj(C_DT, LANES)
        x_st[sp] = x_new

    attn_tasks = ([task_k, task_v] + [task_q(t) for t in range(Q_DIM // W2)]
                  + [task_plain(z_st, C_Z, t) for t in range(SSM_INNER // W2)])
    ssd_tasks = ([task_plain(xraw_st, C_XBC, t) for t in range(CONV_DIM // W2)]
                 + [task_plain(ga_st, C_GA, t) for t in range(D_MODEL // W2)]
                 + [task_plain(gs_st, C_GS, t) for t in range(D_MODEL // W2)])

    def run_share(tasks, part, parts):
        n = len(tasks)
        for t in tasks[part * n // parts:(part + 1) * n // parts]:
            t()

    kd_cur = kd_st[sc]
    vd_cur = vd_st[sc]
    kd_sc[2 * T:3 * T, :] = kd_cur
    vd_sc[2 * T:3 * T, :] = vd_cur

    kd_sc[0:T, :] = jnp.where(jc == 0, kd_cur, kd_sc[0:T, :])
    vd_sc[0:T, :] = jnp.where(jc == 0, vd_cur, vd_sc[0:T, :])

    nq = Q_PER_KV * T
    bias = bias_ref[jnp.minimum(j, bias_ref.shape[0] - 1)]
    rcol = lax.broadcasted_iota(jnp.int32, (nq, 1), 0)

    attn_tiles = []
    for g in range(N_KV_HEADS):
        run_share(attn_tasks, g, N_KV_HEADS)
        qt0 = q_st[sc, :, (2 * g) * LANES:(2 * g + 1) * LANES]
        qt1 = q_st[sc, :, (2 * g + 1) * LANES:(2 * g + 2) * LANES]
        zero = jnp.zeros_like(qt0)
        lhs = jnp.concatenate([jnp.where(lo, qt0, zero), jnp.where(lo, zero, qt0),
                               jnp.where(lo, qt1, zero), jnp.where(lo, zero, qt1)], axis=0)
        s = _dot_nt(lhs, kd_sc[:, g * LANES:(g + 1) * LANES])
        s = (s.reshape(Q_PER_KV, T, 3 * T) + bias[None]).reshape(nq, 3 * T)
        sink = jnp.where(rcol < T, sinks_ref[4 * g],
                         jnp.where(rcol < 2 * T, sinks_ref[4 * g + 1],
                                   jnp.where(rcol < 3 * T, sinks_ref[4 * g + 2], sinks_ref[4 * g + 3])))
        m = jnp.maximum(jnp.max(s, axis=1, keepdims=True), sink)
        p = jnp.exp(s - m)
        den = jnp.sum(p, axis=1, keepdims=True) + jnp.exp(sink - m)
        o = _dot(p.astype(BF16), vd_sc[:, g * LANES:(g + 1) * LANES]) / den
        attn_tiles.append(jnp.where(lo, o[0:T], o[T:2 * T]))
        attn_tiles.append(jnp.where(lo, o[2 * T:3 * T], o[3 * T:4 * T]))
    y_attn = jnp.concatenate(attn_tiles, axis=1)

    kd_sc[T:2 * T, :] = kd_cur
    vd_sc[T:2 * T, :] = vd_cur

    z = z_st[sc]
    xbuf_sc[SUBLANES:SUBLANES + T, :] = xraw_st[sc]
    acc = convb_ref[...]
    for kk in range(CONV_WIDTH):
        acc = acc + xbuf_sc[SUBLANES - kk:SUBLANES - kk + T, :] * convw_ref[CONV_WIDTH - 1 - kk:CONV_WIDTH - kk, :]

    conv_ref[...] = xbuf_sc[SUBLANES + T - (CONV_WIDTH - 1):SUBLANES + T, :]
    xbuf_sc[0:SUBLANES, :] = xbuf_sc[T:T + SUBLANES, :]
    xc = _silu(acc)
    xs = xc[:, :SSM_INNER]
    bm = xc[:, SSM_INNER:SSM_INNER + GN].astype(BF16)
    cm = xc[:, SSM_INNER + GN:].astype(BF16)

    valid = (jc * T + row - META_PAD) >= 0
    dt = jnp.where(valid, _softplus(dt_st[sc] + dtb_ref[...]), 0.0)
    ad = dt * (-jnp.exp(alog_ref[...]))
    tril = row >= lane
    cs = _dot_exact_lhs(tril.astype(BF16), ad)
    cs_t = cs.T

    hp_row = lax.broadcasted_iota(jnp.int32, (LANES, LANES), 0) < SSM_HEAD_DIM
    y_tiles = []
    for i in range(SSM_HEADS // 2):
        run_share(ssd_tasks, i, SSM_HEADS // 2)
        g = i // 2
        h0, h1 = 2 * i, 2 * i + 1
        bg = bm[:, g * SSM_STATE:(g + 1) * SSM_STATE]
        cg = cm[:, g * SSM_STATE:(g + 1) * SSM_STATE]
        if i % 2 == 0:
            cb = _dot_nt(cg, bg)
        col0, col1 = cs[:, h0:h0 + 1], cs[:, h1:h1 + 1]
        row0, row1 = cs_t[h0:h0 + 1, :], cs_t[h1:h1 + 1, :]
        tot0, tot1 = cs_t[h0:h0 + 1, T - 1:T], cs_t[h1:h1 + 1, T - 1:T]
        m0 = (cb * jnp.exp(jnp.where(tril, col0 - row0, NEG_INF))).astype(BF16)
        m1 = (cb * jnp.exp(jnp.where(tril, col1 - row1, NEG_INF))).astype(BF16)
        xs_i = xs[:, i * LANES:(i + 1) * LANES]
        xd = xs_i * jnp.where(lo, dt[:, h0:h0 + 1], dt[:, h1:h1 + 1])
        xdb = xd.astype(BF16)
        y_diag = jnp.where(lo, _dot(m0, xdb), _dot(m1, xdb))
        s_pair = st_sc[i * LANES:(i + 1) * LANES, :]
        y_off = _dot_nt(cg, s_pair.astype(BF16)) * jnp.where(lo, jnp.exp(col0), jnp.exp(col1))
        wgt = jnp.where(lo, jnp.exp(tot0 - col0), jnp.exp(tot1 - col1))
        upd = _dot_tn((xd * wgt).astype(BF16), bg)
        st_sc[i * LANES:(i + 1) * LANES, :] = jnp.where(hp_row, jnp.exp(tot0), jnp.exp(tot1)) * s_pair + upd
        y_tiles.append(y_diag + y_off + dskip_ref[:, i * LANES:(i + 1) * LANES] * xs_i)
    task_tail()
    y = jnp.concatenate(y_tiles, axis=1)
    y_ssd = _gated_rmsnorm(y, z, nw_ref[...])

    mixin = jax.nn.sigmoid(ga_st[sc]) * y_attn + jax.nn.sigmoid(gs_st[sc]) * y_ssd
    mix = _dot(mixin.astype(BF16), wo_ref[...])
    h1 = _layer_norm(ALPHA * x_st[sc] + mix, g1_ref[...], b1_ref[...])
    _to_tiles(h1_ref.at[pl.ds(ph * T * D_TILES, T * D_TILES)], h1)
    rl_ref[ph * T:(ph + 1) * T, :] = _router_logits(h1, wrh_ref[...], wrl_ref[...], br_ref[...])
    for ref in staging:
        ref[sc] = ref[sp]


def _attention_bias():
    t = BLOCK
    rr = jnp.arange(t, dtype=jnp.int32)[:, None]
    cc = jnp.arange(3 * t, dtype=jnp.int32)[None, :]
    ss, kb = cc % t, cc // t
    variants = []
    for jc in (-1, 0, 1, 2):
        mask = (((kb == 0) & (ss >= META_PAD) & ((jc >= 1) | ((jc == 0) & (ss <= rr))))
                | ((kb == 1) & (ss > rr) & (jc >= 2))
                | ((kb == 2) & (ss <= rr) & (jc >= 1)))
        variants.append(jnp.where(mask, 0.0, NEG_INF).astype(F32))
    return jnp.stack(variants)


def _prompt_call(x_prompt, meta_blk, w_in_r, w_o_b, cos_t, sin_t, conv_w, conv_b, dtb, alog, dskip_e, nw,
                 g1, b1, wr_hi, wr_lo, br, sinks):
    B, S, D = x_prompt.shape
    nbs = S // BLOCK
    nb = nbs + 1
    assert (nb + 1) % PHASES == 0 and PHASES == 2
    n_steps = (nb + 1) // PHASES
    tok = lambda ph: (lambda b, s, *_: (b, jnp.clip(PHASES * s + ph - 1, 0, nbs - 1), 0))
    flat2 = lambda b, s, *_: (b * (nbs // PHASES) + jnp.clip(s - 1, 0, nbs // PHASES - 1), 0)
    perb = lambda b, s, *_: (b, 0, 0)
    staged = lambda width, dtype: pltpu.VMEM((2, BLOCK, width), dtype)
    grid_spec = pltpu.PrefetchScalarGridSpec(
        num_scalar_prefetch=1,
        grid=(B, n_steps),
        in_specs=[
            pl.BlockSpec((None, BLOCK, D), tok(0)),
            pl.BlockSpec((None, BLOCK, D), tok(1)),
            _const_spec((BLOCK, D)),
            _const_spec((4, BLOCK, 3 * BLOCK)),
            _const_spec((D, IN_PAD)),
            _const_spec((D, D)),
            _const_spec((nb * BLOCK, LANES)),
            _const_spec((nb * BLOCK, LANES)),
            _const_spec((CONV_WIDTH, CONV_DIM)),
            _const_spec((1, CONV_DIM)),
            _const_spec((1, LANES)),
            _const_spec((1, LANES)),
            _const_spec((1, SSM_INNER)),
            _const_spec((1, SSM_INNER)),
            _const_spec((1, D)),
            _const_spec((1, D)),
            _const_spec((D, LANES)),
            _const_spec((D, LANES)),
            _const_spec((1, LANES)),
        ],
        out_specs=[
            pl.BlockSpec((PHASES * BLOCK * D_TILES, LANES), flat2),
            pl.BlockSpec((PHASES * BLOCK, LANES), flat2),
            pl.BlockSpec((None, N_META, KV_DIM), perb),
            pl.BlockSpec((None, N_META, KV_DIM), perb),
            pl.BlockSpec((None, WINDOW, KV_DIM), perb),
            pl.BlockSpec((None, WINDOW, KV_DIM), perb),
            pl.BlockSpec((None, CONV_WIDTH - 1, CONV_DIM), perb),
            pl.BlockSpec((None, SSM_INNER, SSM_STATE), perb),
        ],
        scratch_shapes=[
            pltpu.VMEM((3 * BLOCK, N_KV_HEADS * LANES), BF16),
            pltpu.VMEM((3 * BLOCK, N_KV_HEADS * LANES), BF16),
            pltpu.VMEM((SUBLANES + BLOCK, CONV_DIM), F32),
            staged(Q_DIM, BF16), staged(N_KV_HEADS * LANES, BF16), staged(N_KV_HEADS * LANES, BF16),
            staged(SSM_INNER, F32), staged(CONV_DIM, F32), staged(LANES, F32),
            staged(D, F32), staged(D, F32), staged(D, F32),
        ],
    )
    n_tok = B * S
    out_shape = [
        jax.ShapeDtypeStruct((n_tok * D_TILES, LANES), F32),
        jax.ShapeDtypeStruct((n_tok, LANES), F32),
        jax.ShapeDtypeStruct((B, N_META, KV_DIM), F32),
        jax.ShapeDtypeStruct((B, N_META, KV_DIM), F32),
        jax.ShapeDtypeStruct((B, WINDOW, KV_DIM), F32),
        jax.ShapeDtypeStruct((B, WINDOW, KV_DIM), F32),
        jax.ShapeDtypeStruct((B, CONV_WIDTH - 1, CONV_DIM), F32),
        jax.ShapeDtypeStruct((B, SSM_INNER, SSM_STATE), F32),
    ]
    return pl.pallas_call(
        _prompt_kernel, grid_spec=grid_spec, out_shape=out_shape, name="prompt_mixer",
        compiler_params=pltpu.CompilerParams(dimension_semantics=("arbitrary", "arbitrary"),
                                             vmem_limit_bytes=VMEM_LIMIT),
    )(sinks, x_prompt, x_prompt, meta_blk, _attention_bias(), w_in_r, w_o_b, cos_t, sin_t, conv_w, conv_b, dtb, alog,
      dskip_e, nw, g1, b1, wr_hi, wr_lo, br)


def _sample_proj_kernel(x_ref, win_ref, cos_ref, sin_ref, h0_ref, h1_ref, h2_ref, convw_ref, convb_ref,
                        dtb_ref, alog_ref, expand_ref,
                        q_ref, k_ref, v_ref, z_ref, xraw_ref, xdt_ref, dec_ref, xs_ref, b_ref, c_ref, ga_ref, gs_ref):
    xb = x_ref[...].astype(BF16)
    cos = cos_ref[...]
    sin = sin_ref[...]
    q_ref[...] = _rope_tiles(_dot(xb, win_ref[:, C_Q:C_K]), cos, sin) * (HEAD_DIM ** -0.5)
    k_ref[...] = _rope_tiles(_dot(xb, win_ref[:, C_K:C_V]), cos, sin)
    v_ref[...] = _dot(xb, win_ref[:, C_V:C_Z])
    z_ref[...] = _dot(xb, win_ref[:, C_Z:C_XBC])
    xraw = _dot(xb, win_ref[:, C_XBC:C_GA])
    xraw_ref[...] = xraw
    acc = (convb_ref[...] + h0_ref[...] * convw_ref[0:1, :] + h1_ref[...] * convw_ref[1:2, :]
           + h2_ref[...] * convw_ref[2:3, :] + xraw * convw_ref[3:4, :])
    xc = _silu(acc)
    xs = xc[:, :SSM_INNER]
    dt = _softplus(_dot(xb, win_ref[:, C_DT:IN_PAD]) + dtb_ref[...])
    ad = dt * (-jnp.exp(alog_ref[...]))
    expand = expand_ref[...]
    dt_e = _dot_exact_rhs(dt, expand)
    ad_e = _dot_exact_rhs(ad, expand)
    xs_ref[...] = xs
    xdt_ref[...] = xs * dt_e
    dec_ref[...] = jnp.exp(ad_e)
    b_ref[...] = xc[:, SSM_INNER:SSM_INNER + GN]
    c_ref[...] = xc[:, SSM_INNER + GN:]
    ga_ref[...] = _dot(xb, win_ref[:, C_GA:C_GS])
    gs_ref[...] = _dot(xb, win_ref[:, C_GS:C_DT])


def _sample_proj_call(xs2d, w_in_r, cos_s, sin_s, hist0, hist1, hist2, conv_w, conv_b, dtb, alog, expand):
    n = xs2d.shape[0]
    shapes = [(n, Q_DIM), (n, KV_DIM), (n, KV_DIM), (n, SSM_INNER), (n, CONV_DIM), (n, SSM_INNER), (n, SSM_INNER),
              (n, SSM_INNER), (n, GN), (n, GN), (n, D_MODEL), (n, D_MODEL)]
    return pl.pallas_call(
        _sample_proj_kernel,
        out_shape=[jax.ShapeDtypeStruct(s, F32) for s in shapes], name="sample_proj",
        compiler_params=pltpu.CompilerParams(vmem_limit_bytes=VMEM_LIMIT),
    )(xs2d, w_in_r, cos_s, sin_s, hist0, hist1, hist2, conv_w, conv_b, dtb, alog, expand)


def _sample_attn_kernel(qblk_ref, mk_ref, mv_ref, wk_ref, wv_ref, nk_ref, nv_ref, sink_ref, o_ref):
    qb = qblk_ref[...]
    qbb = qb.astype(BF16)
    bdot = lambda a, b, dims: lax.dot_general(a, b, dims, preferred_element_type=F32)
    nt = (((2,), (2,)), ((0,), (0,)))
    nn = (((2,), (1,)), ((0,), (0,)))
    wk = wk_ref[...].astype(BF16)
    wv = wv_ref[...].astype(BF16)
    mk = mk_ref[...].astype(BF16)
    mv = mv_ref[...].astype(BF16)
    nk = nk_ref[...]
    nv = nv_ref[...]
    s_win = bdot(qbb, wk, nt)
    s_meta = bdot(qbb, mk, nt)
    s_new = jnp.sum(qb * nk, axis=2, keepdims=True)
    w = wk_ref.shape[1]
    slot = lax.broadcasted_iota(jnp.int32, s_win.shape, 2)
    s_win = jnp.where((w - slot) < WINDOW, s_win, NEG_INF)
    sink = sink_ref[...]
    m = jnp.maximum(jnp.maximum(jnp.max(s_win, axis=2, keepdims=True), jnp.max(s_meta, axis=2, keepdims=True)),
                    jnp.maximum(s_new, sink))
    p_win = jnp.exp(s_win - m)
    p_meta = jnp.exp(s_meta - m)
    p_new = jnp.exp(s_new - m)
    den = (jnp.sum(p_win, axis=2, keepdims=True) + jnp.sum(p_meta, axis=2, keepdims=True) + p_new
           + jnp.exp(sink - m))
    o = (bdot(p_win.astype(BF16), wv, nn) + bdot(p_meta.astype(BF16), mv, nn)
         + p_new * nv)
    o_ref[...] = o / den


def _sample_attn_call(qblk, mk, mv, wk, wv, nk, nv, sink3, nblk):
    n = qblk.shape[0]
    w = wk.shape[1]
    b3 = lambda shape: pl.BlockSpec(shape, lambda i: (i, 0, 0))
    return pl.pallas_call(
        _sample_attn_kernel,
        grid=(n // nblk,),
        in_specs=[b3((nblk, N_HEADS, KV_DIM)), b3((nblk, N_META, KV_DIM)), b3((nblk, N_META, KV_DIM)),
                  b3((nblk, w, KV_DIM)), b3((nblk, w, KV_DIM)), b3((nblk, 1, KV_DIM)), b3((nblk, 1, KV_DIM)),
                  pl.BlockSpec((1, N_HEADS, 1), lambda i: (0, 0, 0))],
        out_specs=b3((nblk, N_HEADS, KV_DIM)),
        out_shape=jax.ShapeDtypeStruct((n, N_HEADS, KV_DIM), F32), name="sample_attn",
        compiler_params=pltpu.CompilerParams(dimension_semantics=("arbitrary",), vmem_limit_bytes=VMEM_LIMIT),
    )(qblk, mk, mv, wk, wv, nk, nv, sink3)


def _sample_ssd_kernel(xdt_ref, dec_ref, b_ref, c_ref, st_ref, yattn_ref, xs_ref, z_ref, ga_ref, gs_ref, x_ref,
                       dskip_ref, nw_ref, wo_ref, g1_ref, b1_ref, wrh_ref, wrl_ref, br_ref,
                       st_out_ref, h1_ref, rl_ref, xdt_t_sc, dec_t_sc, y_t_sc, *, nblk):
    i = pl.program_id(0)
    n_all = xdt_ref.shape[0]

    def hi_lo(a):
        hi = a.astype(BF16)
        return hi, (a - hi.astype(F32)).astype(BF16)

    @pl.when(i == 0)
    def _():
        xdt_t_sc[0], xdt_t_sc[1] = hi_lo(xdt_ref[...].T)
        dec_t_sc[0], dec_t_sc[1] = hi_lo(dec_ref[...].T)
        y_t_sc[...] = jnp.zeros_like(y_t_sc)

    r_id = lax.broadcasted_iota(jnp.int32, (n_all, LANES), 0)
    c_id = lax.broadcasted_iota(jnp.int32, (n_all, LANES), 1)
    for t in range(nblk):
        n = i * nblk + t
        pick_row = (r_id == n).astype(BF16)
        pick_col = (c_id == n).astype(BF16)
        xb_n = _dot(xdt_t_sc[0], pick_row) + _dot(xdt_t_sc[1], pick_row)
        db_n = _dot(dec_t_sc[0], pick_row) + _dot(dec_t_sc[1], pick_row)
        brow = b_ref[pl.ds(n, 1), :]
        crow = c_ref[pl.ds(n, 1), :]
        gw = SSM_INNER // SSM_GROUPS
        bfull = jnp.concatenate([jnp.broadcast_to(brow[:, g * SSM_STATE:(g + 1) * SSM_STATE], (gw, SSM_STATE))
                                 for g in range(SSM_GROUPS)], axis=0)
        cfull = jnp.concatenate([jnp.broadcast_to(crow[:, g * SSM_STATE:(g + 1) * SSM_STATE], (gw, SSM_STATE))
                                 for g in range(SSM_GROUPS)], axis=0)
        h_new = st_ref[t] * db_n + xb_n * bfull
        st_out_ref[t] = h_new
        y_hi, y_lo = hi_lo(h_new * cfull)
        y_t_sc[...] += _dot(y_hi, pick_col) + _dot(y_lo, pick_col)

    @pl.when(i == pl.num_programs(0) - 1)
    def _():
        xs = xs_ref[...]
        y = y_t_sc[...].T + dskip_ref[...] * xs
        y_ssd = _gated_rmsnorm(y, z_ref[...], nw_ref[...])
        mixin = jax.nn.sigmoid(ga_ref[...]) * yattn_ref[...] + jax.nn.sigmoid(gs_ref[...]) * y_ssd
        mix = _dot(mixin.astype(BF16), wo_ref[...])
        h1 = _layer_norm(ALPHA * x_ref[...] + mix, g1_ref[...], b1_ref[...])
        h1_ref[...] = jnp.zeros_like(h1_ref)
        rl_ref[...] = jnp.zeros_like(rl_ref)
        _to_tiles(h1_ref, h1)
        rl_ref[0:n_all, :] = _router_logits(h1, wrh_ref[...], wrl_ref[...], br_ref[...])


def _sample_ssd_call(xdt, dec, bmat, cmat, state3, y_attn, xs, z, ga, gs, x2d, dskip_e, nw, w_o_b, g1, b1,
                     wr_hi, wr_lo, br, nblk):
    n = xdt.shape[0]
    full = lambda a: pl.BlockSpec(a.shape, lambda i: (0,) * a.ndim)
    st_spec = pl.BlockSpec((nblk, SSM_INNER, SSM_STATE), lambda i: (i, 0, 0))
    ins = [xdt, dec, bmat, cmat, state3, y_attn, xs, z, ga, gs, x2d, dskip_e, nw, w_o_b, g1, b1, wr_hi, wr_lo, br]
    in_specs = [full(a) for a in ins]
    in_specs[4] = st_spec
    return pl.pallas_call(
        functools.partial(_sample_ssd_kernel, nblk=nblk),
        grid=(n // nblk,),
        in_specs=in_specs,
        out_specs=[st_spec,
                   pl.BlockSpec((TOKEN_BLOCK * D_TILES, LANES), lambda i: (0, 0)),
                   pl.BlockSpec((TOKEN_BLOCK, LANES), lambda i: (0, 0))],
        out_shape=[jax.ShapeDtypeStruct(state3.shape, F32),
                   jax.ShapeDtypeStruct((TOKEN_BLOCK * D_TILES, LANES), F32),
                   jax.ShapeDtypeStruct((TOKEN_BLOCK, LANES), F32)],
        scratch_shapes=[pltpu.VMEM((2, SSM_INNER, n), BF16), pltpu.VMEM((2, SSM_INNER, n), BF16),
                        pltpu.VMEM((SSM_INNER, n), F32)],
        name="sample_ssd",
        compiler_params=pltpu.CompilerParams(dimension_semantics=("arbitrary",), vmem_limit_bytes=VMEM_LIMIT),
    )(*ins)


def _top2_experts(rl):
    lane = lax.broadcasted_iota(jnp.int32, rl.shape, 1)
    big = jnp.int32(1 << 20)
    is_g = lane < N_EXPERT_GROUPS
    gmax = jnp.max(jnp.where(is_g, rl, -jnp.inf), axis=1, keepdims=True)
    gsel = jnp.min(jnp.where(is_g & (rl == gmax), lane, big), axis=1, keepdims=True)
    gden = jnp.sum(jnp.where(is_g, jnp.exp(rl - gmax), 0.0), axis=1, keepdims=True)
    g_w = 1.0 / gden
    e_id = lane - ROUTE_OFF
    in_grp = (e_id >= 0) & (e_id < N_EXPERTS) & ((e_id // EXPERTS_PER_GROUP) == gsel)
    v1 = jnp.max(jnp.where(in_grp, rl, -jnp.inf), axis=1, keepdims=True)
    i1 = jnp.min(jnp.where(in_grp & (rl == v1), lane, big), axis=1, keepdims=True)
    rest = in_grp & (lane != i1)
    v2 = jnp.max(jnp.where(rest, rl, -jnp.inf), axis=1, keepdims=True)
    i2 = jnp.min(jnp.where(rest & (rl == v2), lane, big), axis=1, keepdims=True)
    e21 = jnp.exp(v2 - v1)
    p1 = g_w / (1.0 + e21)
    p2 = g_w * e21 / (1.0 + e21)
    return i1 - ROUTE_OFF, i2 - ROUTE_OFF, p1, p2


R_E1, R_E2, R_P1, R_P2, R_RANK1, R_RANK2 = range(6)


def _route_kernel(rlp_ref, rls_ref, route_ref, counts_ref, tri_sc, run_sc, *, n_prompt_blocks):
    i = pl.program_id(0)
    tb = rlp_ref.shape[0]

    @pl.when(i == 0)
    def _():
        r = lax.broadcasted_iota(jnp.int32, (tb, tb), 0)
        c = lax.broadcasted_iota(jnp.int32, (tb, tb), 1)
        tri_sc[...] = (r > c).astype(BF16)
        run_sc[...] = jnp.zeros_like(run_sc)

    e1, e2, p1, p2 = _top2_experts(jnp.where(i < n_prompt_blocks, rlp_ref[...], rls_ref[...]))
    lane = lax.broadcasted_iota(jnp.int32, (tb, LANES), 1)
    oh1 = (lane == e1).astype(F32)
    oh2 = (lane == e2).astype(F32)
    both = oh1 + oh2
    before = _dot(tri_sc[...], both.astype(BF16)) + run_sc[...]
    rank1 = jnp.sum(before * oh1, axis=1, keepdims=True)
    rank2 = jnp.sum(before * oh2, axis=1, keepdims=True)
    run_sc[...] += jnp.sum(both, axis=0, keepdims=True)
    rec = jnp.zeros((tb, LANES), F32)
    for k, val in ((R_E1, e1.astype(F32)), (R_E2, e2.astype(F32)), (R_P1, p1), (R_P2, p2),
                   (R_RANK1, rank1), (R_RANK2, rank2)):
        rec = jnp.where(lane == k, val, rec)
    route_ref[...] = rec
    counts_ref[...] = run_sc[...]


def _route_call(rl_p, rl_s):
    tb = TOKEN_BLOCK
    npb = rl_p.shape[0] // tb
    t = rl_p.shape[0] + rl_s.shape[0]
    return pl.pallas_call(
        functools.partial(_route_kernel, n_prompt_blocks=npb),
        grid=(t // tb,),
        in_specs=[pl.BlockSpec((tb, LANES), lambda i: (jnp.minimum(i, npb - 1), 0)),
                  pl.BlockSpec((tb, LANES), lambda i: (0, 0))],
        out_specs=[pl.BlockSpec((tb, LANES), lambda i: (i, 0)), pl.BlockSpec((1, LANES), lambda i: (0, 0))],
        out_shape=[jax.ShapeDtypeStruct((t, LANES), F32), jax.ShapeDtypeStruct((1, LANES), F32)],
        scratch_shapes=[pltpu.VMEM((tb, tb), BF16), pltpu.VMEM((1, LANES), F32)],
        name="moe_route",
        compiler_params=pltpu.CompilerParams(dimension_semantics=("arbitrary",), vmem_limit_bytes=VMEM_LIMIT),
    )(rl_p, rl_s)


def _expert_layout(counts_row, n_tiles):
    tr = EXPERT_ROWS
    counts = counts_row[0, :N_EXPERTS].astype(jnp.int32)
    padded = ((counts + tr - 1) // tr) * tr
    ends = jnp.cumsum(padded)
    starts = ends - padded
    n_used = ends[-1] // tr
    tile_id = jnp.arange(n_tiles, dtype=jnp.int32)
    tile_expert = jnp.sum((jnp.minimum(tile_id, n_used - 1)[:, None] * tr >= ends[None, :]).astype(jnp.int32), axis=1)
    is_last = jnp.any((counts > 0)[None, :] & (tile_id[:, None] == (ends // tr - 1)[None, :]), axis=1)
    zero_tile = (is_last | (tile_id >= n_used)).astype(jnp.int32)
    starts_row = jnp.pad(starts.astype(F32), (0, LANES - N_EXPERTS))[None, :]
    return starts_row, tile_expert, n_used.reshape(1), zero_tile


def _positions_kernel(route_ref, starts_ref, pos_ref):
    rec = route_ref[...]
    lane = lax.broadcasted_iota(jnp.int32, rec.shape, 1)
    starts = starts_ref[...]
    start1 = jnp.sum(jnp.where(lane == rec[:, R_E1:R_E1 + 1].astype(jnp.int32), starts, 0.0), axis=1, keepdims=True)
    start2 = jnp.sum(jnp.where(lane == rec[:, R_E2:R_E2 + 1].astype(jnp.int32), starts, 0.0), axis=1, keepdims=True)
    pos = jnp.where(lane == 0, start1 + rec[:, R_RANK1:R_RANK1 + 1],
                    jnp.where(lane == 1, start2 + rec[:, R_RANK2:R_RANK2 + 1], 0.0))
    pos_ref[...] = pos.T[0:SUBLANES, :].astype(jnp.int32)


def _positions_call(route, starts_row):
    t = route.shape[0]
    tb = TOKEN_BLOCK
    return pl.pallas_call(
        _positions_kernel,
        grid=(t // tb,),
        in_specs=[pl.BlockSpec((tb, LANES), lambda i: (i, 0)), pl.BlockSpec((1, LANES), lambda i: (0, 0))],
        out_specs=pl.BlockSpec((SUBLANES, tb), lambda i: (0, i)),
        out_shape=jax.ShapeDtypeStruct((SUBLANES, t), jnp.int32),
        name="moe_positions",
        compiler_params=pltpu.CompilerParams(dimension_semantics=("arbitrary",)),
    )(route, starts_row)


def _row_copy(src, dst, sem):
    return pltpu.make_async_copy(src, dst, sem)


def _dispatch_kernel(pos1_ref, pos2_ref, zt_ref, hp_ref, hs_ref, xs_hbm, zero_sc, sem, *, n_prompt_blocks):
    i = pl.program_id(0)
    tb = TOKEN_BLOCK
    tr = EXPERT_ROWS

    @pl.when(i == 0)
    def _():
        zero_sc[...] = jnp.zeros_like(zero_sc)

        def ztile(m):
            rows = tr * D_TILES
            return _row_copy(zero_sc, xs_hbm.at[pl.ds(pl.multiple_of(m * rows, rows), rows)], sem)

        def zstart(m, c):
            @pl.when(zt_ref[m] != 0)
            def _():
                ztile(m).start()
            return c

        def zwait(m, c):
            @pl.when(zt_ref[m] != 0)
            def _():
                ztile(m).wait()
            return c

        lax.fori_loop(0, zt_ref.shape[0], zstart, 0)
        lax.fori_loop(0, zt_ref.shape[0], zwait, 0)

    def scatter(src_ref):
        def copies(r):
            t = i * tb + r
            return (_row_copy(src_ref.at[_token_rows(r)], xs_hbm.at[_token_rows(pos1_ref[t])], sem),
                    _row_copy(src_ref.at[_token_rows(r)], xs_hbm.at[_token_rows(pos2_ref[t])], sem))

        def start(r, c):
            a, b = copies(r)
            a.start(priority=0)
            b.start(priority=1)
            return c

        def wait(r, c):
            a, b = copies(r)
            a.wait()
            b.wait()
            return c

        lax.fori_loop(0, tb, start, 0, unroll=8)
        lax.fori_loop(0, tb, wait, 0, unroll=8)

    @pl.when(i < n_prompt_blocks)
    def _():
        scatter(hp_ref)

    @pl.when(i >= n_prompt_blocks)
    def _():
        scatter(hs_ref)


def _dispatch_call(pos1, pos2, zero_tile, h1_p, h1_s, n_tiles):
    tb = TOKEN_BLOCK
    npb = h1_p.shape[0] // (tb * D_TILES)
    t = (h1_p.shape[0] + h1_s.shape[0]) // D_TILES
    grid_spec = pltpu.PrefetchScalarGridSpec(
        num_scalar_prefetch=3,
        grid=(t // tb,),
        in_specs=[pl.BlockSpec((tb * D_TILES, LANES), lambda i, *_: (jnp.minimum(i, npb - 1), 0)),
                  pl.BlockSpec((tb * D_TILES, LANES), lambda i, *_: (0, 0))],
        out_specs=pl.BlockSpec(memory_space=pl.ANY),
        scratch_shapes=[pltpu.VMEM((EXPERT_ROWS * D_TILES, LANES), F32), pltpu.SemaphoreType.DMA(())],
    )
    return pl.pallas_call(
        functools.partial(_dispatch_kernel, n_prompt_blocks=npb), grid_spec=grid_spec,
        out_shape=jax.ShapeDtypeStruct((n_tiles * EXPERT_ROWS * D_TILES, LANES), F32), name="moe_dispatch",
        compiler_params=pltpu.CompilerParams(dimension_semantics=("arbitrary",), vmem_limit_bytes=VMEM_LIMIT),
    )(pos1, pos2, zero_tile, h1_p, h1_s)


def _expert_kernel(te_ref, nu_ref, x_ref, wg_ref, wu_ref, wd_ref, y_ref, wg_sc, wu_sc, wd_sc):
    i = pl.program_id(0)
    used = i < nu_ref[0]

    @pl.when((i == 0) | (te_ref[i] != te_ref[jnp.maximum(i - 1, 0)]))
    def _():
        wg_sc[...] = wg_ref[...].astype(BF16)
        wu_sc[...] = wu_ref[...].astype(BF16)
        wd_sc[...] = wd_ref[...].astype(BF16)

    @pl.when(used)
    def _():
        x = _from_tiles(x_ref, EXPERT_ROWS).astype(BF16)
        hid = _silu(_dot(x, wg_sc[...])) * _dot(x, wu_sc[...])
        _to_tiles(y_ref, _dot(hid.astype(BF16), wd_sc[...]))

    @pl.when(jnp.logical_not(used))
    def _():
        y_ref[...] = jnp.zeros_like(y_ref)


def _expert_call(tile_expert, n_used, xs, wg, wu, wd, n_tiles):
    tr = EXPERT_ROWS
    rows = lambda i, te, nu: (jnp.minimum(i, nu[0] - 1), 0)
    out_rows = lambda i, te, nu: (i, 0)
    wsel = lambda i, te, nu: (te[i], 0, 0)
    grid_spec = pltpu.PrefetchScalarGridSpec(
        num_scalar_prefetch=2,
        grid=(n_tiles,),
        in_specs=[pl.BlockSpec((tr * D_TILES, LANES), rows),
                  pl.BlockSpec((None, D_MODEL, D_EXPERT), wsel),
                  pl.BlockSpec((None, D_MODEL, D_EXPERT), wsel),
                  pl.BlockSpec((None, D_EXPERT, D_MODEL), wsel)],
        out_specs=pl.BlockSpec((tr * D_TILES, LANES), out_rows),
        scratch_shapes=[pltpu.VMEM((D_MODEL, D_EXPERT), BF16), pltpu.VMEM((D_MODEL, D_EXPERT), BF16),
                        pltpu.VMEM((D_EXPERT, D_MODEL), BF16)],
    )
    return pl.pallas_call(
        _expert_kernel, grid_spec=grid_spec,
        out_shape=jax.ShapeDtypeStruct((n_tiles * tr * D_TILES, LANES), F32), name="moe_experts",
        compiler_params=pltpu.CompilerParams(dimension_semantics=("arbitrary",), vmem_limit_bytes=VMEM_LIMIT),
    )(tile_expert, n_used, xs, wg, wu, wd)


def _combine_kernel(pos1_ref, pos2_ref, hp_ref, hs_ref, route_ref, g2_ref, b2_ref, ys_hbm, op_ref, os_ref, ybuf, sem,
                    *, n_prompt_blocks):
    i = pl.program_id(0)
    n = pl.num_programs(0)
    tb = TOKEN_BLOCK

    def copies(blk, slot, r):
        t = blk * tb + r
        return (_row_copy(ys_hbm.at[_token_rows(pos1_ref[t])], ybuf.at[slot, 0, _token_rows(r)], sem.at[slot]),
                _row_copy(ys_hbm.at[_token_rows(pos2_ref[t])], ybuf.at[slot, 1, _token_rows(r)], sem.at[slot]))

    def issue(blk, slot):
        def body(r, c):
            a, b = copies(blk, slot, r)
            a.start(priority=0)
            b.start(priority=1)
            return c
        lax.fori_loop(0, tb, body, 0, unroll=8)

    def drain(blk, slot):
        def body(r, c):
            a, b = copies(blk, slot, r)
            a.wait()
            b.wait()
            return c
        lax.fori_loop(0, tb, body, 0, unroll=8)

    @pl.when(i == 0)
    def _():
        issue(0, 0)

    @pl.when(i + 1 < n)
    def _():
        issue(i + 1, (i + 1) % 2)

    slot = i % 2
    drain(i, slot)
    route = route_ref[...]
    p1 = route[:, R_P1:R_P1 + 1]
    p2 = route[:, R_P2:R_P2 + 1]
    h = jnp.where(i < n_prompt_blocks, _from_tiles(hp_ref, tb), _from_tiles(hs_ref, tb))
    moe = p1 * _from_tiles(ybuf.at[slot, 0], tb) + p2 * _from_tiles(ybuf.at[slot, 1], tb)
    out = _layer_norm(ALPHA * h + moe, g2_ref[...], b2_ref[...])

    @pl.when(i < n_prompt_blocks)
    def _():
        op_ref[...] = out

    @pl.when(i >= n_prompt_blocks)
    def _():
        os_ref[...] = out


def _combine_call(pos1, pos2, h1_p, h1_s, route, g2, b2, ys):
    n_prompt_tokens = h1_p.shape[0] // D_TILES
    t = n_prompt_tokens + h1_s.shape[0] // D_TILES
    tb = TOKEN_BLOCK
    npb = n_prompt_tokens // tb
    grid_spec = pltpu.PrefetchScalarGridSpec(
        num_scalar_prefetch=2,
        grid=(t // tb,),
        in_specs=[pl.BlockSpec((tb * D_TILES, LANES), lambda i, *_: (jnp.minimum(i, npb - 1), 0)),
                  pl.BlockSpec((tb * D_TILES, LANES), lambda i, *_: (0, 0)),
                  pl.BlockSpec((tb, LANES), lambda i, *_: (i, 0)),
                  pl.BlockSpec((1, D_MODEL), lambda i, *_: (0, 0)),
                  pl.BlockSpec((1, D_MODEL), lambda i, *_: (0, 0)),
                  pl.BlockSpec(memory_space=pl.ANY)],
        out_specs=[pl.BlockSpec((tb, D_MODEL), lambda i, *_: (jnp.minimum(i, npb - 1), 0)),
                   pl.BlockSpec((tb, D_MODEL), lambda i, *_: (0, 0))],
        scratch_shapes=[pltpu.VMEM((2, TOP_K, tb * D_TILES, LANES), F32), pltpu.SemaphoreType.DMA((2,))],
    )
    return pl.pallas_call(
        functools.partial(_combine_kernel, n_prompt_blocks=npb), grid_spec=grid_spec,
        out_shape=[jax.ShapeDtypeStruct((n_prompt_tokens, D_MODEL), F32),
                   jax.ShapeDtypeStruct((tb, D_MODEL), F32)],
        name="moe_combine",
        compiler_params=pltpu.CompilerParams(dimension_semantics=("arbitrary",), vmem_limit_bytes=VMEM_LIMIT),
    )(pos1, pos2, h1_p, h1_s, route, g2, b2, ys)


def _rope_tables(pos):
    inv_freq = jnp.exp(-math.log(ROPE_THETA) * jnp.arange(HALF, dtype=F32) * (2.0 / HEAD_DIM))
    ang = pos.astype(F32)[:, None] * inv_freq[None, :]
    cos = jnp.cos(ang)
    sin = jnp.sin(ang)
    cos_t = jnp.tile(jnp.concatenate([cos, cos], axis=1), (1, LANES // HEAD_DIM))
    sin_t = jnp.tile(jnp.concatenate([-sin, sin], axis=1), (1, LANES // HEAD_DIM))
    return cos_t, sin_t


def _pad_lanes(a, n=LANES):
    return jnp.pad(a, ((0, 0), (0, n - a.shape[1])))


def kernel(x_prompt, x_sample, cache_meta_k, cache_meta_v, cache_win_k, cache_win_v, state_conv, state_ssm,
           meta_tokens, w_in, attn_sinks, conv_w, conv_b, dt_bias, a_log, d_skip, ssm_norm_w, w_o,
           ln1_g, ln1_b, router_group_w, router_group_b, router_expert_w, router_expert_b,
           w_gate, w_up, w_down, ln2_g, ln2_b):
    B, S, D = x_prompt.shape
    N = x_sample.shape[0]
    l = 0

    wi = w_in[l].astype(BF16)
    dt0 = C_GA
    w_in_r = jnp.concatenate([wi[:, :dt0], wi[:, dt0 + SSM_HEADS:], wi[:, dt0:dt0 + SSM_HEADS],
                              jnp.zeros((D, LANES - SSM_HEADS), wi.dtype)], axis=1)
    w_o_b = w_o[l].astype(BF16)
    wr = _pad_lanes(jnp.concatenate([router_group_w[l], router_expert_w[l]], axis=1))
    wr_hi = wr.astype(BF16)
    wr_lo = (wr - wr_hi.astype(F32)).astype(BF16)
    br = _pad_lanes(jnp.concatenate([router_group_b[l], router_expert_b[l]])[None, :])
    dtb = _pad_lanes(dt_bias[l][None, :])
    alog = _pad_lanes(a_log[l][None, :])
    dskip_e = jnp.repeat(d_skip[l], SSM_HEAD_DIM)[None, :]
    nw = ssm_norm_w[l][None, :]
    g1, b1 = ln1_g[l][None, :], ln1_b[l][None, :]
    g2, b2 = ln2_g[l][None, :], ln2_b[l][None, :]
    cw_, cb_ = conv_w[l], conv_b[l][None, :]
    sinks = attn_sinks[l]
    wg_b, wu_b, wd_b = w_gate[l], w_up[l], w_down[l]

    nb = S // BLOCK + 1
    pos_p = jnp.arange(nb * BLOCK, dtype=jnp.int32) - META_PAD
    cos_p, sin_p = _rope_tables(pos_p)
    meta_blk = jnp.concatenate([jnp.zeros((META_PAD, D), F32), meta_tokens.astype(F32)], axis=0)
    (h1_p, rl_p, p_mk, p_mv, p_wk, p_wv, p_conv, p_ssm) = _prompt_call(
        x_prompt, meta_blk, w_in_r, w_o_b, cos_p, sin_p, cw_, cb_, dtb, alog, dskip_e, nw, g1, b1,
        wr_hi, wr_lo, br, sinks)

    xs2d = x_sample.reshape(N, D)
    cos_s, sin_s = _rope_tables(jnp.full((1,), PAST_LEN, jnp.int32))
    sc = state_conv[l]
    expand = (jnp.arange(LANES, dtype=jnp.int32)[:, None]
              == (jnp.arange(SSM_INNER, dtype=jnp.int32) // SSM_HEAD_DIM)[None, :]).astype(BF16)
    (q_s, k_s, v_s, z_s, xraw_s, xdt_s, dec_s, xsc_s, b_s, c_s, ga_s, gs_s) = _sample_proj_call(
        xs2d, w_in_r, cos_s, sin_s, sc[:, 0], sc[:, 1], sc[:, 2], cw_, cb_, dtb, alog, expand)
    head_group = jnp.arange(N_HEADS, dtype=jnp.int32) // Q_PER_KV
    grp_mask = (head_group[:, None] == jnp.arange(N_KV_HEADS, dtype=jnp.int32)[None, :]).astype(F32)
    qblk = (q_s.reshape(N, N_HEADS, 1, HEAD_DIM) * grp_mask[None, :, :, None]).reshape(N, N_HEADS, KV_DIM)
    wlen = cache_win_k.shape[2]
    o_blk = _sample_attn_call(
        qblk, cache_meta_k[l].reshape(N, N_META, KV_DIM), cache_meta_v[l].reshape(N, N_META, KV_DIM),
        cache_win_k[l].reshape(N, wlen, KV_DIM), cache_win_v[l].reshape(N, wlen, KV_DIM),
        k_s.reshape(N, 1, KV_DIM), v_s.reshape(N, 1, KV_DIM), sinks.reshape(1, N_HEADS, 1), 16)
    o4 = o_blk.reshape(N, N_HEADS, N_KV_HEADS, HEAD_DIM)
    y_attn_s = jnp.sum(o4 * grp_mask[None, :, :, None], axis=2).reshape(N, Q_DIM)
    s_ssm3, h1_s, rl_s = _sample_ssd_call(
        xdt_s, dec_s, b_s, c_s, state_ssm[l].reshape(N, SSM_INNER, SSM_STATE), y_attn_s, xsc_s, z_s, ga_s, gs_s,
        xs2d, dskip_e, nw, w_o_b, g1, b1, wr_hi, wr_lo, br, 8)

    n_tok = rl_p.shape[0] + rl_s.shape[0]
    n_tiles = (TOP_K * n_tok) // EXPERT_ROWS + N_EXPERTS
    route, counts = _route_call(rl_p, rl_s)
    starts_row, tile_expert, n_used, zero_tile = _expert_layout(counts, n_tiles)
    pos = _positions_call(route, starts_row)
    pos1, pos2 = pos[0], pos[1]
    xs_sorted = _dispatch_call(pos1, pos2, zero_tile, h1_p, h1_s, n_tiles)
    ys_sorted = _expert_call(tile_expert, n_used, xs_sorted, wg_b, wu_b, wd_b, n_tiles)
    y_p2d, y_s2d = _combine_call(pos1, pos2, h1_p, h1_s, route, g2, b2, ys_sorted)
    y_prompt = y_p2d.reshape(B, S, D)
    y_sample = y_s2d[:N].reshape(N, 1, D)

    kv5 = lambda a, n, t: a.reshape(1, n, t, N_KV_HEADS, HEAD_DIM)
    s_conv = jnp.stack([sc[:, 1], sc[:, 2], xraw_s], axis=1)[None]
    return (y_prompt, y_sample,
            kv5(p_mk, B, N_META), kv5(p_mv, B, N_META), kv5(p_wk, B, WINDOW), kv5(p_wv, B, WINDOW),
            p_conv[None], p_ssm.reshape(1, B, SSM_HEADS, SSM_HEAD_DIM, SSM_STATE),
            kv5(k_s, N, 1), kv5(v_s, N, 1), s_conv,
            s_ssm3.reshape(1, N, SSM_HEADS, SSM_HEAD_DIM, SSM_STATE))
```
